```python
import math
import jax, jax.numpy as jnp
from jax import lax
import numpy as np

D_MODEL = 2048
BATCH = 4
SEQ = 8192
DEPTH = 1

MLA_HEADS = 8
MLA_NOPE = 128
MLA_ROPE = 64
MLA_QK = MLA_NOPE + MLA_ROPE
MLA_V = 128
MLA_Q_RANK = 768
MLA_KV_RANK = 512
MLA_WIDTH = MLA_HEADS * MLA_V
ROPE_THETA = 10000.0
Q_BLOCK = 128
MOBA_HEADS = 8
MOBA_HEAD_DIM = 128
MOBA_WIDTH = MOBA_HEADS * MOBA_HEAD_DIM
MOBA_BLOCK = 256
MOBA_TOPK = 3
MOBA_Q_CHUNK = 16
MIX_WIDTH = MLA_WIDTH + MOBA_WIDTH
IN_SPLITS = [MLA_Q_RANK, MLA_KV_RANK, MLA_ROPE, MOBA_WIDTH, MOBA_WIDTH, MOBA_WIDTH]
IN_COLS = sum(IN_SPLITS)
PEER_HEADS = 8
PEER_NKEYS = 128
PEER_EXPERTS = PEER_NKEYS * PEER_NKEYS
PEER_DKEY = 256
PEER_TOPK = 16
PEER_TOKEN_CHUNK = 128
EPS = 1e-6
NEG = -1e30

kernel_name = 'hymba_mla_moba_peer_layer'


def rmsnorm(x, g):
    xf = x.astype(jnp.float32)
    y = xf * lax.rsqrt(jnp.mean(xf * xf, axis=-1, keepdims=True) + EPS)
    return (y * g.astype(jnp.float32)).astype(x.dtype)


def rope_tables(seq):
    half = MLA_ROPE // 2
    inv_freq = ROPE_THETA ** (-jnp.arange(half, dtype=jnp.float32) / half)
    ang = jnp.arange(seq, dtype=jnp.float32)[:, None] * inv_freq[None, :]
    return jnp.cos(ang), jnp.sin(ang)


def apply_rope(x, cos, sin):
    half = x.shape[-1] // 2
    x1 = x[..., :half].astype(jnp.float32)
    x2 = x[..., half:].astype(jnp.float32)
    return jnp.concatenate([x1 * cos - x2 * sin, x2 * cos + x1 * sin], axis=-1).astype(x.dtype)


def causal_block_attention(q, k, v, scale):
    B, H, S, Dk = q.shape
    Dv = v.shape[-1]
    nblk = S // Q_BLOCK
    qb = q.reshape(B, H, nblk, Q_BLOCK, Dk).transpose(2, 0, 1, 3, 4)
    kpos = jnp.arange(S)

    def one(args):
        qi, i = args
        s = jnp.einsum('bhqd,bhkd->bhqk', qi, k).astype(jnp.float32) * scale
        qpos = i * Q_BLOCK + jnp.arange(Q_BLOCK)
        s = jnp.where(kpos[None, :] <= qpos[:, None], s, NEG)
        p = jax.nn.softmax(s, axis=-1)
        return jnp.einsum('bhqk,bhkd->bhqd', p.astype(v.dtype), v)

    out = lax.map(one, (qb, jnp.arange(nblk)))
    return out.transpose(1, 2, 0, 3, 4).reshape(B, H, S, Dv)


def mla_group(q_lat, kv_lat, k_rope, g_q, w_uq, g_kv, w_ukv):
    B, S, _ = q_lat.shape
    cos, sin = rope_tables(S)
    q = (rmsnorm(q_lat, g_q) @ w_uq).reshape(B, S, MLA_HEADS, MLA_QK).transpose(0, 2, 1, 3)
    q = jnp.concatenate([q[..., :MLA_NOPE], apply_rope(q[..., MLA_NOPE:], cos, sin)], axis=-1)
    kv = (rmsnorm(kv_lat, g_kv) @ w_ukv).reshape(B, S, MLA_HEADS, MLA_NOPE + MLA_V).transpose(0, 2, 1, 3)
    k_r = apply_rope(k_rope, cos, sin)
    k = jnp.concatenate([kv[..., :MLA_NOPE],
                         jnp.broadcast_to(k_r[:, None], (B, MLA_HEADS, S, MLA_ROPE))], axis=-1)
    v = kv[..., MLA_NOPE:]
    o = causal_block_attention(q, k, v, 1.0 / math.sqrt(MLA_QK))
    return o.transpose(0, 2, 1, 3).reshape(B, S, MLA_WIDTH)


def moba_group(q_in, k_in, v_in):
    B, S, _ = q_in.shape
    H, Dh, L, C = MOBA_HEADS, MOBA_HEAD_DIM, MOBA_BLOCK, MOBA_Q_CHUNK
    q = q_in.reshape(B, S, H, Dh).transpose(0, 2, 1, 3)
    k = k_in.reshape(B, S, H, Dh).transpose(0, 2, 1, 3)
    v = v_in.reshape(B, S, H, Dh).transpose(0, 2, 1, 3)
    scale = 1.0 / math.sqrt(Dh)
    slopes = 2.0 ** (-8.0 * (jnp.arange(H, dtype=jnp.float32) + 1.0) / H)
    nb = -(-S // L)
    pad = nb * L - S
    kp = jnp.pad(k, ((0, 0), (0, 0), (0, pad), (0, 0)))
    vp = jnp.pad(v, ((0, 0), (0, 0), (0, pad), (0, 0)))
    kb = kp.reshape(B, H, nb, L, Dh)
    vb = vp.reshape(B, H, nb, L, Dh)
    kmean = jnp.mean(kb.astype(jnp.float32), axis=3)
    pos = jnp.arange(S)
    qblk = pos // L
    gate = jnp.einsum('bhsd,bhnd->bhsn', q.astype(jnp.float32), kmean)
    gate = jnp.where(jnp.arange(nb)[None, :] < qblk[:, None], gate, NEG)
    k_sel = min(MOBA_TOPK, nb)
    _, sel = lax.top_k(gate, k_sel)
    valid = jnp.arange(k_sel)[None, :] < qblk[:, None]
    nc = S // C
    qc = q.reshape(B, H, nc, C, Dh).transpose(2, 0, 1, 3, 4)
    selc = sel.reshape(B, H, nc, C, k_sel).transpose(2, 0, 1, 3, 4)
    validc = valid.reshape(nc, C, k_sel)
    bi = jnp.arange(B)[:, None, None, None]
    hi = jnp.arange(H)[None, :, None, None]
    offs = jnp.arange(L)

    def one(args):
        qi, si, vi, ci = args
        qpos = ci * C + jnp.arange(C)
        own = (ci * C) // L
        kg = kb[bi, hi, si]
        vg = vb[bi, hi, si]
        s_past = jnp.einsum('bhcd,bhcnld->bhcnl', qi, kg).astype(jnp.float32) * scale
        kpos_past = si[..., None] * L + offs
        s_past = s_past - slopes[None, :, None, None, None] * (qpos[None, None, :, None, None] - kpos_past)
        s_past = jnp.where(vi[None, None, :, :, None], s_past, NEG).reshape(B, H, C, k_sel * L)
        ko = lax.dynamic_slice_in_dim(kp, own * L, L, axis=2)
        vo = lax.dynamic_slice_in_dim(vp, own * L, L, axis=2)
        kpos_own = own * L + offs
        s_own = jnp.einsum('bhcd,bhld->bhcl', qi, ko).astype(jnp.float32) * scale
        s_own = s_own - slopes[None, :, None, None] * (qpos[:, None] - kpos_own[None, :])
        s_own = jnp.where(kpos_own[None, :] <= qpos[:, None], s_own, NEG)
        p = jax.nn.softmax(jnp.concatenate([s_past, s_own], axis=-1), axis=-1).astype(v.dtype)
        p_past = p[..., :k_sel * L].reshape(B, H, C, k_sel, L)
        p_own = p[..., k_sel * L:]
        return (jnp.einsum('bhcnl,bhcnld->bhcd', p_past, vg)
                + jnp.einsum('bhcl,bhld->bhcd', p_own, vo))

    out = lax.map(one, (qc, selc, validc, jnp.arange(nc)))
    out = out.transpose(1, 2, 0, 3, 4).reshape(B, H, S, Dh)
    return out.transpose(0, 2, 1, 3).reshape(B, S, MOBA_WIDTH)


def peer(h, w_q, sub_k1, sub_k2, u_tab, v_tab):
    B, S, D = h.shape
    T = B * S
    H, K = PEER_HEADS, PEER_TOPK
    hf = h.reshape(T, D)
    q = (hf @ w_q).reshape(T, H, 2, PEER_DKEY // 2)
    s1 = jnp.einsum('thd,hnd->thn', q[:, :, 0], sub_k1).astype(jnp.float32)
    s2 = jnp.einsum('thd,hnd->thn', q[:, :, 1], sub_k2).astype(jnp.float32)
    v1, i1 = lax.top_k(s1, K)
    v2, i2 = lax.top_k(s2, K)
    cand = (v1[..., :, None] + v2[..., None, :]).reshape(T, H, K * K)
    cand_id = (i1[..., :, None] * PEER_NKEYS + i2[..., None, :]).reshape(T, H, K * K)
    top, slot = lax.top_k(cand, K)
    ids = jnp.take_along_axis(cand_id, slot, axis=-1).reshape(T, H * K)
    g = jax.nn.softmax(top, axis=-1).reshape(T, H * K).astype(h.dtype)
    nchunk = T // PEER_TOKEN_CHUNK

    def one(args):
        hc, ic, gc = args
        a = jax.nn.gelu(jnp.einsum('cd,ckd->ck', hc, u_tab[ic]), approximate=False) * gc
        return jnp.einsum('ck,ckd->cd', a, v_tab[ic])

    out = lax.map(one, (hf.reshape(nchunk, PEER_TOKEN_CHUNK, D),
                        ids.reshape(nchunk, PEER_TOKEN_CHUNK, H * K),
                        g.reshape(nchunk, PEER_TOKEN_CHUNK, H * K)))
    return out.reshape(B, S, D)


def setup_inputs(seed: int = 0) -> dict:
    key = jax.random.key(seed)
    ks = jax.random.split(key, 22)
    L = DEPTH

    def nrm(k, shape, scale):
        return jax.random.normal(k, shape, jnp.float32) * scale

    def gain(k, shape):
        return 1.0 + 0.02 * jax.random.normal(k, shape, jnp.float32)

    return {
        'x': nrm(ks[0], (BATCH, SEQ, D_MODEL), 1.0),
        'c': nrm(ks[1], (BATCH, D_MODEL), 1.0),
        'w_ada': nrm(ks[2], (L, D_MODEL, 6 * D_MODEL), 0.5 * D_MODEL ** -0.5),
        'b_ada': nrm(ks[3], (L, 6 * D_MODEL), 0.01),
        'g_pre_mix': gain(ks[4], (L, D_MODEL)),
        'w_in': nrm(ks[5], (L, D_MODEL, IN_COLS), D_MODEL ** -0.5),
        'g_q_lat': gain(ks[6], (L, MLA_Q_RANK)),
        'w_uq': nrm(ks[7], (L, MLA_Q_RANK, MLA_HEADS * MLA_QK), MLA_Q_RANK ** -0.5),
        'g_kv_lat': gain(ks[8], (L, MLA_KV_RANK)),
        'w_ukv': nrm(ks[9], (L, MLA_KV_RANK, MLA_HEADS * (MLA_NOPE + MLA_V)), MLA_KV_RANK ** -0.5),
        'g_out_mla': gain(ks[10], (L, MLA_WIDTH)),
        'g_out_moba': gain(ks[11], (L, MOBA_WIDTH)),
        'w_out': nrm(ks[12], (L, MIX_WIDTH, D_MODEL), MIX_WIDTH ** -0.5),
        'g_post_mix': gain(ks[13], (L, D_MODEL)),
        'g_pre_ffn': gain(ks[14], (L, D_MODEL)),
        'w_peer_q': nrm(ks[15], (L, D_MODEL, PEER_HEADS * PEER_DKEY), D_MODEL ** -0.5),
        'sub_keys_1': nrm(ks[16], (L, PEER_HEADS, PEER_NKEYS, PEER_DKEY // 2), (PEER_DKEY // 2) ** -0.5),
        'sub_keys_2': nrm(ks[17], (L, PEER_HEADS, PEER_NKEYS, PEER_DKEY // 2), (PEER_DKEY // 2) ** -0.5),
        'u_experts': nrm(ks[18], (L, PEER_EXPERTS, D_MODEL), D_MODEL ** -0.5),
        'v_experts': nrm(ks[19], (L, PEER_EXPERTS, D_MODEL), (PEER_HEADS * PEER_TOPK) ** -0.5),
        'g_post_ffn': gain(ks[20], (L, D_MODEL)),
    }


def reference(x, c, w_ada, b_ada, g_pre_mix, w_in, g_q_lat, w_uq, g_kv_lat, w_ukv,
              g_out_mla, g_out_moba, w_out, g_post_mix, g_pre_ffn, w_peer_q,
              sub_keys_1, sub_keys_2, u_experts, v_experts, g_post_ffn):
    B = x.shape[0]
    cs = jax.nn.silu(c)
    offsets = list(np.cumsum(IN_SPLITS)[:-1])
    for l in range(DEPTH):
        mod = (cs @ w_ada[l] + b_ada[l]).reshape(B, 6, 1, D_MODEL)
        shift1, scale1, gate1 = mod[:, 0], mod[:, 1], mod[:, 2]
        shift2, scale2, gate2 = mod[:, 3], mod[:, 4], mod[:, 5]
        h = rmsnorm(x, g_pre_mix[l]) * (1.0 + scale1) + shift1
        proj = h @ w_in[l]
        q_lat, kv_lat, k_rope, mq, mk, mv = jnp.split(proj, offsets, axis=-1)
        o_mla = mla_group(q_lat, kv_lat, k_rope, g_q_lat[l], w_uq[l], g_kv_lat[l], w_ukv[l])
        o_moba = moba_group(mq, mk, mv)
        o = jnp.concatenate([rmsnorm(o_mla, g_out_mla[l]), rmsnorm(o_moba, g_out_moba[l])], axis=-1)
        x = x + gate1 * rmsnorm(o @ w_out[l], g_post_mix[l])
        h2 = rmsnorm(x, g_pre_ffn[l]) * (1.0 + scale2) + shift2
        y2 = peer(h2, w_peer_q[l], sub_keys_1[l], sub_keys_2[l], u_experts[l], v_experts[l])
        x = x + gate2 * rmsnorm(y2, g_post_ffn[l])
    return x
```

```python
import functools
import math

import jax
import jax.numpy as jnp
from jax import lax
from jax.experimental import pallas as pl
from jax.experimental.pallas import tpu as pltpu

F32 = jnp.float32
BF16 = jnp.bfloat16

D_MODEL = 2048
MLA_HEADS = 8
MLA_NOPE = 128
MLA_ROPE = 64
MLA_QK = MLA_NOPE + MLA_ROPE
MLA_V = 128
MLA_Q_RANK = 768
MLA_KV_RANK = 512
MLA_HEAD_PAD = 256
ROPE_THETA = 10000.0
MOBA_HEADS = 8
MOBA_HEAD_DIM = 128
MOBA_WIDTH = MOBA_HEADS * MOBA_HEAD_DIM
MOBA_BLOCK = 256
MOBA_TOPK = 3
PEER_HEADS = 8
PEER_NKEYS = 128
PEER_EXPERTS = PEER_NKEYS * PEER_NKEYS
PEER_DKEY = 256
PEER_TOPK = 16
EPS = 1e-6
NEG = -1e30
LANES = 128
VMEM_LIMIT = 56 * 1024 * 1024


def _cparams(n_axes, vmem=VMEM_LIMIT):
    return pltpu.CompilerParams(dimension_semantics=("arbitrary",) * n_axes,
                                vmem_limit_bytes=vmem)


def _rms(x):
    return x * lax.rsqrt(jnp.mean(x * x, axis=-1, keepdims=True) + EPS)


def _ada_kernel(c_ref, w_ref, b_ref, o_ref):
    c = c_ref[...]
    cs = c / (1.0 + jnp.exp(-c))
    o_ref[...] = jnp.dot(cs, w_ref[...], preferred_element_type=F32) + b_ref[...]


def _ada(c, w, b):
    bsz, d = c.shape
    n = w.shape[1]
    rows = 8
    cp = jnp.zeros((rows, d), F32).at[:bsz].set(c)
    tn = 1024
    out = pl.pallas_call(
        _ada_kernel,
        out_shape=jax.ShapeDtypeStruct((rows, n), F32),
        grid=(n // tn,),
        in_specs=[pl.BlockSpec((rows, d), lambda j: (0, 0)),
                  pl.BlockSpec((d, tn), lambda j: (0, j)),
                  pl.BlockSpec((1, tn), lambda j: (0, j))],
        out_specs=pl.BlockSpec((rows, tn), lambda j: (0, j)),
        compiler_params=_cparams(1),
        name="ada_mod",
    )(cp, w, b.reshape(1, n))
    return out[:bsz].reshape(bsz, 6, d)


def _prenorm_kernel(x_ref, g_ref, mod_ref, h_ref):
    y = _rms(x_ref[...]) * g_ref[...]
    h_ref[...] = (y * (1.0 + mod_ref[0, 1:2, :]) + mod_ref[0, 0:1, :]).astype(h_ref.dtype)


def _prenorm(x2, g, mod, seq, tm):
    t, d = x2.shape
    per_b = seq // tm
    return pl.pallas_call(
        _prenorm_kernel,
        out_shape=jax.ShapeDtypeStruct((t, d), BF16),
        grid=(t // tm,),
        in_specs=[pl.BlockSpec((tm, d), lambda i: (i, 0)),
                  pl.BlockSpec((1, d), lambda i: (0, 0)),
                  pl.BlockSpec((1, 6, d), lambda i: (i // per_b, 0, 0))],
        out_specs=pl.BlockSpec((tm, d), lambda i: (i, 0)),
        compiler_params=_cparams(1),
        name="prenorm_mod",
    )(x2, g.reshape(1, d), mod)


def _mm_kernel(a_ref, w_ref, o_ref):
    o_ref[...] = jnp.dot(a_ref[...], w_ref[...], preferred_element_type=F32).astype(o_ref.dtype)


def _matmul(a, w, tm, tn, name):
    t, k = a.shape
    n = w.shape[1]
    return pl.pallas_call(
        _mm_kernel,
        out_shape=jax.ShapeDtypeStruct((t, n), BF16),
        grid=(t // tm, n // tn),
        in_specs=[pl.BlockSpec((tm, k), lambda i, j: (i, 0)),
                  pl.BlockSpec((k, tn), lambda i, j: (0, j))],
        out_specs=pl.BlockSpec((tm, tn), lambda i, j: (i, j)),
        compiler_params=_cparams(2),
        name=name,
    )(a, w)


def _rope_half(r, cc, ss):
    return r * cc + pltpu.roll(r, 64, 1) * ss


def _mla_q_kernel(h_ref, wl_ref, g_ref, wu_ref, cc_ref, ss_ref, q_ref):
    ql = jnp.dot(h_ref[...], wl_ref[...], preferred_element_type=F32)
    qn = (_rms(ql) * g_ref[...]).astype(BF16)
    q2 = jnp.dot(qn, wu_ref[...], preferred_element_type=F32)
    cc = cc_ref[...]
    ss = ss_ref[...]
    for h in range(MLA_HEADS):
        base = h * MLA_HEAD_PAD
        q_ref[:, base:base + MLA_NOPE] = q2[:, base:base + MLA_NOPE].astype(BF16)
        r = q2[:, base + MLA_NOPE:base + MLA_HEAD_PAD]
        q_ref[:, base + MLA_NOPE:base + MLA_HEAD_PAD] = _rope_half(r, cc, ss).astype(BF16)


def _mla_q(h, wl, g, wu, cc, ss, seq, tm):
    t, d = h.shape
    per_b = seq // tm
    n = MLA_HEADS * MLA_HEAD_PAD
    return pl.pallas_call(
        _mla_q_kernel,
        out_shape=jax.ShapeDtypeStruct((t, n), BF16),
        grid=(t // tm,),
        in_specs=[pl.BlockSpec((tm, d), lambda i: (i, 0)),
                  pl.BlockSpec((d, MLA_Q_RANK), lambda i: (0, 0)),
                  pl.BlockSpec((1, MLA_Q_RANK), lambda i: (0, 0)),
                  pl.BlockSpec((MLA_Q_RANK, n), lambda i: (0, 0)),
                  pl.BlockSpec((tm, LANES), lambda i: (i % per_b, 0)),
                  pl.BlockSpec((tm, LANES), lambda i: (i % per_b, 0))],
        out_specs=pl.BlockSpec((tm, n), lambda i: (i, 0)),
        compiler_params=_cparams(1),
        name="mla_q_path",
    )(h, wl, g.reshape(1, MLA_Q_RANK), wu, cc, ss)


def _mla_kv_kernel(h_ref, wl_ref, g_ref, wu_ref, cc_ref, ss_ref, k_ref, v_ref):
    kl = jnp.dot(h_ref[...], wl_ref[...], preferred_element_type=F32)
    kvn = (_rms(kl[:, :MLA_KV_RANK]) * g_ref[...]).astype(BF16)
    kr = _rope_half(kl[:, MLA_KV_RANK:], cc_ref[...], ss_ref[...]).astype(BF16)
    kv2 = jnp.dot(kvn, wu_ref[...], preferred_element_type=F32)
    for h in range(MLA_HEADS):
        base = h * MLA_HEAD_PAD
        k_ref[:, base:base + MLA_NOPE] = kv2[:, h * MLA_NOPE:(h + 1) * MLA_NOPE].astype(BF16)
        k_ref[:, base + MLA_NOPE:base + MLA_HEAD_PAD] = kr
    v_ref[...] = kv2[:, MLA_HEADS * MLA_NOPE:].astype(BF16)


def _mla_kv(h, wl, g, wu, cc, ss, seq, tm):
    t, d = h.shape
    per_b = seq // tm
    nl = MLA_KV_RANK + LANES
    nk = MLA_HEADS * MLA_HEAD_PAD
    nv = MLA_HEADS * MLA_V
    return pl.pallas_call(
        _mla_kv_kernel,
        out_shape=(jax.ShapeDtypeStruct((t, nk), BF16), jax.ShapeDtypeStruct((t, nv), BF16)),
        grid=(t // tm,),
        in_specs=[pl.BlockSpec((tm, d), lambda i: (i, 0)),
                  pl.BlockSpec((d, nl), lambda i: (0, 0)),
                  pl.BlockSpec((1, MLA_KV_RANK), lambda i: (0, 0)),
                  pl.BlockSpec((MLA_KV_RANK, nk), lambda i: (0, 0)),
                  pl.BlockSpec((tm, LANES), lambda i: (i % per_b, 0)),
                  pl.BlockSpec((tm, LANES), lambda i: (i % per_b, 0))],
        out_specs=(pl.BlockSpec((tm, nk), lambda i: (i, 0)),
                   pl.BlockSpec((tm, nv), lambda i: (i, 0))),
        compiler_params=_cparams(1),
        name="mla_kv_path",
    )(h, wl, g.reshape(1, MLA_KV_RANK), wu, cc, ss)


def _online_update(carry, s, v):
    m, l, acc = carry
    m_new = jnp.maximum(m, jnp.max(s, axis=-1, keepdims=True))
    alpha = jnp.exp(m - m_new)
    p = jnp.exp(s - m_new)
    l = alpha * l + jnp.sum(p, axis=-1, keepdims=True)
    acc = alpha * acc + jnp.dot(p.astype(BF16), v, preferred_element_type=F32)
    return m_new, l, acc


def _qk(q, k):
    return lax.dot_general(q, k, (((1,), (1,)), ((), ())), preferred_element_type=F32)


def _mla_attn_kernel(q_ref, k_ref, v_ref, o_ref, *, tq, tk):
    i = pl.program_id(2)
    q = q_ref[...]
    ratio = tq // tk
    n_full = i * ratio

    def kv_chunk(j):
        start = pl.multiple_of(j * tk, tk)
        return k_ref[pl.ds(start, tk), :], v_ref[pl.ds(start, tk), :]

    def full_body(j, carry):
        k, v = kv_chunk(j)
        return _online_update(carry, _qk(q, k), v)

    carry = (jnp.full((tq, 1), NEG, F32), jnp.zeros((tq, 1), F32), jnp.zeros((tq, MLA_V), F32))
    carry = lax.fori_loop(0, n_full, full_body, carry)
    row = lax.broadcasted_iota(jnp.int32, (tq, tk), 0)
    col = lax.broadcasted_iota(jnp.int32, (tq, tk), 1)
    for d in range(ratio):
        k, v = kv_chunk(n_full + d)
        s = jnp.where(row >= col + d * tk, _qk(q, k), NEG)
        carry = _online_update(carry, s, v)
    _, l, acc = carry
    o_ref[...] = (acc / l).astype(o_ref.dtype)


def _mla_attn(q, k, v, bsz, seq, tq, tk):
    nq = seq // tq
    return pl.pallas_call(
        functools.partial(_mla_attn_kernel, tq=tq, tk=tk),
        out_shape=jax.ShapeDtypeStruct((bsz * seq, MLA_HEADS * MLA_V), BF16),
        grid=(bsz, MLA_HEADS, nq),
        in_specs=[pl.BlockSpec((tq, MLA_HEAD_PAD), lambda b, h, i: (b * nq + i, h)),
                  pl.BlockSpec((seq, MLA_HEAD_PAD), lambda b, h, i: (b, h)),
                  pl.BlockSpec((seq, MLA_V), lambda b, h, i: (b, h))],
        out_specs=pl.BlockSpec((tq, MLA_V), lambda b, h, i: (b * nq + i, h)),
        compiler_params=_cparams(3),
        name="mla_attention",
    )(q, k, v)


def _moba_kernel(q_ref, k_ref, v_ref, slope_ref, o_ref, kmean_sc, *, n_blocks):
    L = MOBA_BLOCK
    i = pl.program_id(2)

    @pl.when(i == 0)
    def _():
        kmean_sc[...] = jnp.zeros_like(kmean_sc)
        for n in range(n_blocks):
            kb = k_ref[n * L:(n + 1) * L, :].astype(F32)
            kmean_sc[n:n + 1, :] = jnp.sum(kb, axis=0, keepdims=True) * (1.0 / L)

    q = q_ref[...]
    lane = lax.broadcasted_iota(jnp.int32, (L, LANES), 1)
    gate = jnp.where(lane < i, _qk(q, kmean_sc[...].astype(BF16)), NEG)
    sel = jnp.zeros((L, LANES), F32)
    for _ in range(MOBA_TOPK):
        m = jnp.max(gate, axis=-1, keepdims=True)
        first = jnp.min(jnp.where(gate == m, lane, LANES), axis=-1, keepdims=True)
        hit = lane == first
        sel = jnp.where(hit, jnp.where(m > 0.5 * NEG, 1.0, 0.0), sel)
        gate = jnp.where(hit, -jnp.inf, gate)

    slope = slope_ref[0]
    colf = lax.broadcasted_iota(jnp.int32, (1, L), 1).astype(F32)
    row = lax.broadcasted_iota(jnp.int32, (L, L), 0)
    col = lax.broadcasted_iota(jnp.int32, (L, L), 1)

    def kv_block(j):
        start = pl.multiple_of(j * L, L)
        return k_ref[pl.ds(start, L), :], v_ref[pl.ds(start, L), :]

    k, v = kv_block(i)
    s = jnp.where(row >= col, _qk(q, k) + slope * colf, NEG)
    carry = (jnp.full((L, 1), NEG, F32), jnp.zeros((L, 1), F32), jnp.zeros((L, MOBA_HEAD_DIM), F32))
    carry = _online_update(carry, s, v)

    def past_body(j, carry):
        k, v = kv_block(j)
        picked = jnp.sum(jnp.where(lane == j, sel, 0.0), axis=-1, keepdims=True)
        bias = slope * (colf + ((j - i) * L).astype(F32))
        s = jnp.where(picked > 0.5, _qk(q, k) + bias, NEG)
        return _online_update(carry, s, v)

    _, l, acc = lax.fori_loop(0, i, past_body, carry)
    o_ref[...] = (acc / l).astype(o_ref.dtype)


def _moba_attn(qkv, slopes, bsz, seq):
    L = MOBA_BLOCK
    nq = seq // L
    hh = MOBA_HEADS
    return pl.pallas_call(
        functools.partial(_moba_kernel, n_blocks=nq),
        out_shape=jax.ShapeDtypeStruct((bsz * seq, MOBA_WIDTH), BF16),
        grid=(bsz, hh, nq),
        in_specs=[pl.BlockSpec((L, MOBA_HEAD_DIM), lambda b, h, i: (b * nq + i, h)),
                  pl.BlockSpec((seq, MOBA_HEAD_DIM), lambda b, h, i: (b, hh + h)),
                  pl.BlockSpec((seq, MOBA_HEAD_DIM), lambda b, h, i: (b, 2 * hh + h)),
                  pl.BlockSpec((1, 1, L), lambda b, h, i: (h, 0, 0))],
        out_specs=pl.BlockSpec((L, MOBA_HEAD_DIM), lambda b, h, i: (b * nq + i, h)),
        scratch_shapes=[pltpu.VMEM((LANES, MOBA_HEAD_DIM), F32)],
        compiler_params=_cparams(3),
        name="moba_attention",
    )(qkv, qkv, qkv, slopes)


def _outproj_kernel(oa_ref, ob_ref, ga_ref, gb_ref, w_ref, gpost_ref, x_ref, mod_ref, gffn_ref,
                    x1_ref, h2_ref, h2t_ref):
    half = oa_ref.shape[1]
    a = (_rms(oa_ref[...].astype(F32)) * ga_ref[...]).astype(BF16)
    b = (_rms(ob_ref[...].astype(F32)) * gb_ref[...]).astype(BF16)
    y = jnp.dot(a, w_ref[:half, :], preferred_element_type=F32)
    y = y + jnp.dot(b, w_ref[half:, :], preferred_element_type=F32)
    x1 = x_ref[...] + mod_ref[0, 2:3, :] * (_rms(y) * gpost_ref[...])
    x1_ref[...] = x1
    h2 = _rms(x1) * gffn_ref[...] * (1.0 + mod_ref[0, 4:5, :]) + mod_ref[0, 3:4, :]
    h2_ref[...] = h2.astype(BF16)
    h2t_ref[...] = h2.T.astype(BF16)


def _outproj(oa, ob, ga, gb, w, gpost, x2, mod, gffn, seq, tm):
    t, d = x2.shape
    half = oa.shape[1]
    per_b = seq // tm
    row = lambda n: pl.BlockSpec((1, n), lambda i: (0, 0))
    return pl.pallas_call(
        _outproj_kernel,
        out_shape=(jax.ShapeDtypeStruct((t, d), F32), jax.ShapeDtypeStruct((t, d), BF16),
                   jax.ShapeDtypeStruct((d, t), BF16)),
        grid=(t // tm,),
        in_specs=[pl.BlockSpec((tm, half), lambda i: (i, 0)),
                  pl.BlockSpec((tm, half), lambda i: (i, 0)),
                  row(half), row(half),
                  pl.BlockSpec((2 * half, d), lambda i: (0, 0)),
                  row(d),
                  pl.BlockSpec((tm, d), lambda i: (i, 0)),
                  pl.BlockSpec((1, 6, d), lambda i: (i // per_b, 0, 0)),
                  row(d)],
        out_specs=(pl.BlockSpec((tm, d), lambda i: (i, 0)),
                   pl.BlockSpec((tm, d), lambda i: (i, 0)),
                   pl.BlockSpec((d, tm), lambda i: (0, i))),
        compiler_params=_cparams(1),
        name="out_proj",
    )(oa, ob, ga.reshape(1, half), gb.reshape(1, half), w, gpost.reshape(1, d), x2, mod,
      gffn.reshape(1, d))


def _peer_select_kernel(q_ref, k1_ref, k2_ref, s1_ref, s2_ref, st_ref):
    tm = q_ref.shape[0]
    hh, kk = PEER_HEADS, PEER_TOPK
    rowi = lax.broadcasted_iota(jnp.int32, (PEER_NKEYS, tm), 0)
    tops = ([], [])
    for h in range(hh):
        for p, (kref, sref) in enumerate(((k1_ref, s1_ref), (k2_ref, s2_ref))):
            c0 = (2 * h + p) * (PEER_DKEY // 2)
            s = _qk(kref[h], q_ref[:, c0:c0 + PEER_DKEY // 2])
            sref[h] = s
            vals = []
            for _ in range(kk):
                m = jnp.max(s, axis=0, keepdims=True)
                first = jnp.min(jnp.where(s == m, rowi, PEER_NKEYS), axis=0, keepdims=True)
                s = jnp.where(rowi == first, -jnp.inf, s)
                vals.append(m)
            tops[p].append(vals)
    v1 = [jnp.concatenate([tops[0][h][a] for h in range(hh)], axis=0) for a in range(kk)]
    v2 = [jnp.concatenate([tops[1][h][a] for h in range(hh)], axis=0) for a in range(kk)]
    cands = [v1[a] + v2[b] for a in range(kk) for b in range(kk) if (a + 1) * (b + 1) <= kk]
    best = v1[0] + v2[0]
    z = jnp.zeros_like(best)
    cur = best
    for r in range(kk):
        cur = functools.reduce(jnp.maximum, cands)
        z = z + jnp.exp(cur - best)
        if r + 1 < kk:
            todo = jnp.ones_like(best)
            nxt = []
            for c in cands:
                hit = jnp.where(c == cur, todo, 0.0)
                nxt.append(jnp.where(hit > 0.5, -jnp.inf, c))
                todo = todo - hit
            cands = nxt
    st_ref[0 * hh:1 * hh, :] = cur
    st_ref[1 * hh:2 * hh, :] = v1[0]
    st_ref[2 * hh:3 * hh, :] = v2[0]
    st_ref[3 * hh:4 * hh, :] = 1.0 / z


def _peer_select(q, k1, k2, tm):
    t, d = q.shape
    hh, nk = PEER_HEADS, PEER_NKEYS
    keys = pl.BlockSpec((hh, nk, PEER_DKEY // 2), lambda i: (0, 0, 0))
    sc = pl.BlockSpec((hh, nk, tm), lambda i: (0, 0, i))
    return pl.pallas_call(
        _peer_select_kernel,
        out_shape=(jax.ShapeDtypeStruct((hh, nk, t), F32), jax.ShapeDtypeStruct((hh, nk, t), F32),
                   jax.ShapeDtypeStruct((4 * hh, t), F32)),
        grid=(t // tm,),
        in_specs=[pl.BlockSpec((tm, d), lambda i: (i, 0)), keys, keys],
        out_specs=(sc, sc, pl.BlockSpec((4 * hh, tm), lambda i: (0, i))),
        compiler_params=_cparams(1),
        name="peer_select",
    )(q, k1, k2)


def _peer_main_kernel(h2t_ref, u_ref, vt_ref, s1_ref, s2_ref, st_ref, x1_ref, mod_ref, g_ref,
                      o_ref, acc_sc, e2_sc, c1_sc, p_sc, *, eb):
    hh, nk = PEER_HEADS, PEER_NKEYS
    e = pl.program_id(1)
    groups = eb // nk

    @pl.when(e == 0)
    def _():
        acc_sc[...] = jnp.zeros_like(acc_sc)
        for h in range(hh):
            e2_sc[h] = jnp.exp(s2_ref[h] - st_ref[2 * hh + h:2 * hh + h + 1, :])
            c1_sc[h] = (jnp.exp(s1_ref[h] - st_ref[hh + h:hh + h + 1, :])
                        * st_ref[3 * hh + h:3 * hh + h + 1, :])

    a = jnp.dot(u_ref[...], h2t_ref[...], preferred_element_type=F32)
    for g in range(groups):
        i1 = e * groups + g
        w = jnp.zeros((nk, a.shape[1]), F32)
        for h in range(hh):
            pair = s2_ref[h] + s1_ref[h, pl.ds(i1, 1), :]
            w = w + jnp.where(pair >= st_ref[h:h + 1, :], e2_sc[h], 0.0) * c1_sc[h, pl.ds(i1, 1), :]
        ag = a[g * nk:(g + 1) * nk, :]
        gelu = 0.5 * ag * (1.0 + lax.erf(ag * (1.0 / math.sqrt(2.0))))
        p_sc[g * nk:(g + 1) * nk, :] = (gelu * w).astype(BF16)
    acc_sc[...] += jnp.dot(vt_ref[...], p_sc[...], preferred_element_type=F32)

    @pl.when(e == pl.num_programs(1) - 1)
    def _():
        y = acc_sc[...].T
        o_ref[...] = x1_ref[...] + mod_ref[0, 5:6, :] * (_rms(y) * g_ref[...])


def _peer_main(h2t, u, vt, s1t, s2t, st, x1, mod, g, seq, tm, eb):
    d, t = h2t.shape
    hh, nk = PEER_HEADS, PEER_NKEYS
    per_b = seq // tm
    sc = pl.BlockSpec((hh, nk, tm), lambda i, e: (0, 0, i))
    return pl.pallas_call(
        functools.partial(_peer_main_kernel, eb=eb),
        out_shape=jax.ShapeDtypeStruct((t, d), F32),
        grid=(t // tm, PEER_EXPERTS // eb),
        in_specs=[pl.BlockSpec((d, tm), lambda i, e: (0, i)),
                  pl.BlockSpec((eb, d), lambda i, e: (e, 0)),
                  pl.BlockSpec((d, eb), lambda i, e: (0, e)),
                  sc, sc,
                  pl.BlockSpec((4 * hh, tm), lambda i, e: (0, i)),
                  pl.BlockSpec((tm, d), lambda i, e: (i, 0)),
                  pl.BlockSpec((1, 6, d), lambda i, e: (i // per_b, 0, 0)),
                  pl.BlockSpec((1, d), lambda i, e: (0, 0))],
        out_specs=pl.BlockSpec((tm, d), lambda i, e: (i, 0)),
        scratch_shapes=[pltpu.VMEM((d, tm), F32),
                        pltpu.VMEM((hh, nk, tm), F32),
                        pltpu.VMEM((hh, nk, tm), F32),
                        pltpu.VMEM((eb, tm), BF16)],
        compiler_params=_cparams(2),
        name="peer_experts",
    )(h2t, u, vt, s1t, s2t, st, x1, mod, g.reshape(1, d))


def _rope_tables(seq):
    half = MLA_ROPE // 2
    inv_freq = ROPE_THETA ** (-jnp.arange(half, dtype=F32) / half)
    ang = jnp.arange(seq, dtype=F32)[:, None] * inv_freq[None, :]
    zeros = jnp.zeros((seq, LANES - MLA_ROPE), F32)
    cc = jnp.concatenate([jnp.cos(ang), jnp.cos(ang), zeros], axis=-1)
    ss = jnp.concatenate([jnp.sin(ang), jnp.sin(ang), zeros], axis=-1)
    return cc, ss


def _rot_cols(w):
    half = MLA_ROPE // 2
    return jnp.concatenate([-w[..., half:], w[..., :half]], axis=-1)


def _layer(x2, mod, bsz, seq, w_in, g_pre_mix, g_q_lat, w_uq, g_kv_lat, w_ukv, g_out_mla,
           g_out_moba, w_out, g_post_mix, g_pre_ffn, w_peer_q, sub_keys_1, sub_keys_2,
           u_experts, v_experts, g_post_ffn):
    t = bsz * seq
    tm = min(512, seq)

    o_q, o_kv, o_kr, o_mq, o_mk, o_mv = 0, 768, 1280, 1344, 2368, 3392
    w_ql = w_in[:, o_q:o_kv].astype(BF16)
    w_kr = w_in[:, o_kr:o_mq]
    w_kvl = jnp.concatenate([w_in[:, o_kv:o_kr], w_kr, _rot_cols(w_kr)], axis=-1).astype(BF16)
    moba_scale = 1.0 / math.sqrt(MOBA_HEAD_DIM)
    w_m = jnp.concatenate([w_in[:, o_mq:o_mk] * moba_scale, w_in[:, o_mk:]], axis=-1).astype(BF16)
    wq3 = w_uq.reshape(MLA_Q_RANK, MLA_HEADS, MLA_QK) * (1.0 / math.sqrt(MLA_QK))
    wq_rope = wq3[..., MLA_NOPE:]
    w_uq_p = jnp.concatenate([wq3[..., :MLA_NOPE], wq_rope, _rot_cols(wq_rope)], axis=-1)
    w_uq_p = w_uq_p.reshape(MLA_Q_RANK, MLA_HEADS * MLA_HEAD_PAD).astype(BF16)
    wkv3 = w_ukv.reshape(MLA_KV_RANK, MLA_HEADS, MLA_NOPE + MLA_V)
    w_ukv_p = jnp.concatenate([wkv3[..., :MLA_NOPE].reshape(MLA_KV_RANK, -1),
                               wkv3[..., MLA_NOPE:].reshape(MLA_KV_RANK, -1)], axis=-1).astype(BF16)
    cc, ss = _rope_tables(seq)
    slopes = 2.0 ** (-8.0 * (jnp.arange(MOBA_HEADS, dtype=F32) + 1.0) / MOBA_HEADS)
    slopes = jnp.broadcast_to(slopes[:, None, None], (MOBA_HEADS, 1, MOBA_BLOCK))

    h = _prenorm(x2, g_pre_mix, mod, seq, tm)
    q = _mla_q(h, w_ql, g_q_lat, w_uq_p, cc, ss, seq, tm)
    k, v = _mla_kv(h, w_kvl, g_kv_lat, w_ukv_p, cc, ss, seq, tm)
    qkv_m = _matmul(h, w_m, min(1024, t), 1024, "moba_qkv_proj")
    o_mla = _mla_attn(q, k, v, bsz, seq, min(512, seq), min(512, seq))
    o_moba = _moba_attn(qkv_m, slopes, bsz, seq)
    x1, h2, h2t = _outproj(o_mla, o_moba, g_out_mla, g_out_moba, w_out.astype(BF16), g_post_mix,
                           x2, mod, g_pre_ffn, seq, tm)

    pq = _matmul(h2, w_peer_q.astype(BF16), min(1024, t), 1024, "peer_q_proj")
    s1t, s2t, st = _peer_select(pq, sub_keys_1.astype(BF16), sub_keys_2.astype(BF16), min(256, t))
    return _peer_main(h2t, u_experts.astype(BF16), v_experts.T.astype(BF16), s1t, s2t, st, x1, mod,
                      g_post_ffn, seq, tm, 512)


def kernel(x, c, w_ada, b_ada, g_pre_mix, w_in, g_q_lat, w_uq, g_kv_lat, w_ukv, g_out_mla,
           g_out_moba, w_out, g_post_mix, g_pre_ffn, w_peer_q, sub_keys_1, sub_keys_2, u_experts,
           v_experts, g_post_ffn):
    bsz, seq, d = x.shape
    x2 = x.reshape(bsz * seq, d)
    for l in range(w_ada.shape[0]):
        mod = _ada(c, w_ada[l], b_ada[l])
        x2 = _layer(x2, mod, bsz, seq, w_in[l], g_pre_mix[l], g_q_lat[l], w_uq[l], g_kv_lat[l],
                    w_ukv[l], g_out_mla[l], g_out_moba[l], w_out[l], g_post_mix[l], g_pre_ffn[l],
                    w_peer_q[l], sub_keys_1[l], sub_keys_2[l], u_experts[l], v_experts[l],
                    g_post_ffn[l])
    return x2.reshape(bsz, seq, d)
```

```python
import functools
import math

import jax
import jax.numpy as jnp
from jax import lax
from jax.experimental import pallas as pl
from jax.experimental.pallas import tpu as pltpu

F32 = jnp.float32
BF16 = jnp.bfloat16

D_MODEL = 2048
MLA_HEADS = 8
MLA_NOPE = 128
MLA_ROPE = 64
MLA_QK = MLA_NOPE + MLA_ROPE
MLA_V = 128
MLA_Q_RANK = 768
MLA_KV_RANK = 512
MLA_HEAD_PAD = 256
ROPE_THETA = 10000.0
MOBA_HEADS = 8
MOBA_HEAD_DIM = 128
MOBA_WIDTH = MOBA_HEADS * MOBA_HEAD_DIM
MOBA_BLOCK = 256
MOBA_TOPK = 3
PEER_HEADS = 8
PEER_NKEYS = 128
PEER_EXPERTS = PEER_NKEYS * PEER_NKEYS
PEER_DKEY = 256
PEER_TOPK = 16
EPS = 1e-6
NEG = -1e30
LOG2E = 1.4426950408889634
MOBA_GROUP = 8
MOBA_SUB = 2
KEY_CHUNK = 256
LANES = 128
VMEM_LIMIT = 56 * 1024 * 1024


def _cparams(n_axes, vmem=VMEM_LIMIT):
    return pltpu.CompilerParams(dimension_semantics=("arbitrary",) * n_axes,
                                vmem_limit_bytes=vmem)


def _rms(x):
    return x * lax.rsqrt(jnp.mean(x * x, axis=-1, keepdims=True) + EPS)


def _ada_kernel(c_ref, w_ref, b_ref, o_ref):
    c = c_ref[...]
    cs = c / (1.0 + jnp.exp(-c))
    o_ref[...] = jnp.dot(cs, w_ref[...], preferred_element_type=F32) + b_ref[...]


def _ada(c, w, b):
    bsz, d = c.shape
    n = w.shape[1]
    rows = 8
    cp = jnp.zeros((rows, d), F32).at[:bsz].set(c)
    tn = 1024
    out = pl.pallas_call(
        _ada_kernel,
        out_shape=jax.ShapeDtypeStruct((rows, n), F32),
        grid=(n // tn,),
        in_specs=[pl.BlockSpec((rows, d), lambda j: (0, 0)),
                  pl.BlockSpec((d, tn), lambda j: (0, j)),
                  pl.BlockSpec((1, tn), lambda j: (0, j))],
        out_specs=pl.BlockSpec((rows, tn), lambda j: (0, j)),
        compiler_params=_cparams(1),
        name="ada_mod",
    )(cp, w, b.reshape(1, n))
    return out[:bsz].reshape(bsz, 6, d)


def _prenorm_kernel(x_ref, g_ref, mod_ref, h_ref):
    y = _rms(x_ref[...]) * g_ref[...]
    h_ref[...] = (y * (1.0 + mod_ref[0, 1:2, :]) + mod_ref[0, 0:1, :]).astype(h_ref.dtype)


def _prenorm(x2, g, mod, seq, tm):
    t, d = x2.shape
    per_b = seq // tm
    return pl.pallas_call(
        _prenorm_kernel,
        out_shape=jax.ShapeDtypeStruct((t, d), BF16),
        grid=(t // tm,),
        in_specs=[pl.BlockSpec((tm, d), lambda i: (i, 0)),
                  pl.BlockSpec((1, d), lambda i: (0, 0)),
                  pl.BlockSpec((1, 6, d), lambda i: (i // per_b, 0, 0))],
        out_specs=pl.BlockSpec((tm, d), lambda i: (i, 0)),
        compiler_params=_cparams(1),
        name="prenorm_mod",
    )(x2, g.reshape(1, d), mod)


def _mm_kernel(a_ref, w_ref, o_ref):
    o_ref[...] = jnp.dot(a_ref[...], w_ref[...], preferred_element_type=F32).astype(o_ref.dtype)


def _matmul(a, w, tm, tn, name):
    t, k = a.shape
    n = w.shape[1]
    return pl.pallas_call(
        _mm_kernel,
        out_shape=jax.ShapeDtypeStruct((t, n), BF16),
        grid=(t // tm, n // tn),
        in_specs=[pl.BlockSpec((tm, k), lambda i, j: (i, 0)),
                  pl.BlockSpec((k, tn), lambda i, j: (0, j))],
        out_specs=pl.BlockSpec((tm, tn), lambda i, j: (i, j)),
        compiler_params=_cparams(2),
        name=name,
    )(a, w)


def _rope_half(r, cc, ss):
    return r * cc + pltpu.roll(r, 64, 1) * ss


def _mla_q_kernel(h_ref, wl_ref, g_ref, wu_ref, cc_ref, ss_ref, q_ref):
    ql = jnp.dot(h_ref[...], wl_ref[...], preferred_element_type=F32)
    qn = (_rms(ql) * g_ref[...]).astype(BF16)
    q2 = jnp.dot(qn, wu_ref[...], preferred_element_type=F32)
    cc = cc_ref[...]
    ss = ss_ref[...]
    for h in range(MLA_HEADS):
        base = h * MLA_HEAD_PAD
        q_ref[:, base:base + MLA_NOPE] = q2[:, base:base + MLA_NOPE].astype(BF16)
        r = q2[:, base + MLA_NOPE:base + MLA_HEAD_PAD]
        q_ref[:, base + MLA_NOPE:base + MLA_HEAD_PAD] = _rope_half(r, cc, ss).astype(BF16)


def _mla_q(h, wl, g, wu, cc, ss, seq, tm):
    t, d = h.shape
    per_b = seq // tm
    n = MLA_HEADS * MLA_HEAD_PAD
    return pl.pallas_call(
        _mla_q_kernel,
        out_shape=jax.ShapeDtypeStruct((t, n), BF16),
        grid=(t // tm,),
        in_specs=[pl.BlockSpec((tm, d), lambda i: (i, 0)),
                  pl.BlockSpec((d, MLA_Q_RANK), lambda i: (0, 0)),
                  pl.BlockSpec((1, MLA_Q_RANK), lambda i: (0, 0)),
                  pl.BlockSpec((MLA_Q_RANK, n), lambda i: (0, 0)),
                  pl.BlockSpec((tm, LANES), lambda i: (i % per_b, 0)),
                  pl.BlockSpec((tm, LANES), lambda i: (i % per_b, 0))],
        out_specs=pl.BlockSpec((tm, n), lambda i: (i, 0)),
        compiler_params=_cparams(1),
        name="mla_q_path",
    )(h, wl, g.reshape(1, MLA_Q_RANK), wu, cc, ss)


def _mla_kv_kernel(h_ref, wl_ref, g_ref, wu_ref, cc_ref, ss_ref, k_ref, vt_ref):
    kl = jnp.dot(h_ref[...], wl_ref[...], preferred_element_type=F32)
    kvn = (_rms(kl[:, :MLA_KV_RANK]) * g_ref[...]).astype(BF16)
    kr = _rope_half(kl[:, MLA_KV_RANK:], cc_ref[...], ss_ref[...]).astype(BF16)
    kv2 = jnp.dot(kvn, wu_ref[...], preferred_element_type=F32)
    for h in range(MLA_HEADS):
        base = h * MLA_HEAD_PAD
        k_ref[:, base:base + MLA_NOPE] = kv2[:, h * MLA_NOPE:(h + 1) * MLA_NOPE].astype(BF16)
        k_ref[:, base + MLA_NOPE:base + MLA_HEAD_PAD] = kr
    for c in range(vt_ref.shape[2]):
        v = kv2[c * KEY_CHUNK:(c + 1) * KEY_CHUNK, MLA_HEADS * MLA_NOPE:]
        vt_ref[0, :, c] = v.T.reshape(MLA_HEADS, MLA_V, KEY_CHUNK).astype(BF16)


def _mla_kv(h, wl, g, wu, cc, ss, bsz, seq, tm):
    t, d = h.shape
    per_b = seq // tm
    cpt = tm // KEY_CHUNK
    nl = MLA_KV_RANK + LANES
    nk = MLA_HEADS * MLA_HEAD_PAD
    vt_shape = (bsz, MLA_HEADS, seq // KEY_CHUNK, MLA_V, KEY_CHUNK)
    return pl.pallas_call(
        _mla_kv_kernel,
        out_shape=(jax.ShapeDtypeStruct((t, nk), BF16), jax.ShapeDtypeStruct(vt_shape, BF16)),
        grid=(t // tm,),
        in_specs=[pl.BlockSpec((tm, d), lambda i: (i, 0)),
                  pl.BlockSpec((d, nl), lambda i: (0, 0)),
                  pl.BlockSpec((1, MLA_KV_RANK), lambda i: (0, 0)),
                  pl.BlockSpec((MLA_KV_RANK, nk), lambda i: (0, 0)),
                  pl.BlockSpec((tm, LANES), lambda i: (i % per_b, 0)),
                  pl.BlockSpec((tm, LANES), lambda i: (i % per_b, 0))],
        out_specs=(pl.BlockSpec((tm, nk), lambda i: (i, 0)),
                   pl.BlockSpec((1, MLA_HEADS, cpt, MLA_V, KEY_CHUNK),
                                lambda i: (i // per_b, 0, i % per_b, 0, 0))),
        compiler_params=_cparams(1),
        name="mla_kv_path",
    )(h, wl, g.reshape(1, MLA_KV_RANK), wu, cc, ss)


def _online_update_t(carry, parts, vts, offset):
    m, l, acc = carry
    mb = functools.reduce(jnp.maximum, [jnp.max(x, axis=0, keepdims=True) for x in parts])
    m_new = jnp.maximum(m, mb + offset)
    alpha = jnp.exp2(m - m_new)
    shift = offset - m_new
    l = alpha * l
    acc = alpha * acc
    for x, vt in zip(parts, vts):
        p = jnp.exp2(x + shift)
        l = l + jnp.sum(p, axis=0, keepdims=True)
        acc = acc + jnp.dot(vt, p.astype(BF16), preferred_element_type=F32)
    return m_new, l, acc


def _attn_init(tq, dv):
    return (jnp.full((1, tq), NEG, F32), jnp.zeros((1, tq), F32), jnp.zeros((dv, tq), F32))


def _mla_attn_kernel(q_ref, k_ref, vt_ref, o_ref, *, tq):
    C = KEY_CHUNK
    per = tq // C
    i = pl.program_id(2)
    qt = q_ref[...].astype(F32).T.astype(BF16)
    zero = jnp.zeros((1, tq), F32)

    def scores(c):
        k = k_ref[pl.ds(pl.multiple_of(c * C, C), C), :]
        return jnp.dot(k, qt, preferred_element_type=F32)

    def trip(t, carry):
        tiles = [scores(t * per + u) for u in range(per)]
        for u in range(per):
            carry = _online_update_t(carry, [tiles[u]], [vt_ref[0, 0, t * per + u]], zero)
        return carry

    carry = lax.fori_loop(0, i, trip, _attn_init(tq, MLA_V))
    key = lax.broadcasted_iota(jnp.int32, (C, tq), 0)
    qry = lax.broadcasted_iota(jnp.int32, (C, tq), 1)
    for u in range(per):
        c = i * per + u
        s = jnp.where(key + u * C <= qry, scores(c), NEG)
        carry = _online_update_t(carry, [s], [vt_ref[0, 0, c]], zero)
    _, l, acc = carry
    o_ref[...] = (acc / l).T.astype(o_ref.dtype)


def _mla_attn(q, k, vt, bsz, seq, tq):
    nq = seq // tq
    nc = seq // KEY_CHUNK
    return pl.pallas_call(
        functools.partial(_mla_attn_kernel, tq=tq),
        out_shape=jax.ShapeDtypeStruct((bsz * seq, MLA_HEADS * MLA_V), BF16),
        grid=(bsz, MLA_HEADS, nq),
        in_specs=[pl.BlockSpec((tq, MLA_HEAD_PAD), lambda b, h, i: (b * nq + i, h)),
                  pl.BlockSpec((seq, MLA_HEAD_PAD), lambda b, h, i: (b, h)),
                  pl.BlockSpec((1, 1, nc, MLA_V, KEY_CHUNK), lambda b, h, i: (b, h, 0, 0, 0))],
        out_specs=pl.BlockSpec((tq, MLA_V), lambda b, h, i: (b * nq + i, h)),
        compiler_params=_cparams(3),
        name="mla_attention",
    )(q, k, vt)


def _moba_kernel(q_ref, k_ref, vt_ref, slope_ref, o_ref, kmean_sc, sel_sc, bias_sc, *,
                 n_blocks, group, sub):
    L = MOBA_BLOCK
    nbp = kmean_sc.shape[0]
    i = pl.program_id(2)
    slope = slope_ref[0]

    @pl.when(i == 0)
    def _():
        kmean_sc[...] = jnp.zeros_like(kmean_sc)
        for n in range(n_blocks):
            kb = k_ref[n * L:(n + 1) * L, :].astype(F32)
            kmean_sc[n:n + 1, :] = jnp.sum(kb, axis=0, keepdims=True) * (1.0 / L)
        bias_sc[...] = slope * lax.broadcasted_iota(jnp.int32, bias_sc.shape, 0).astype(F32)

    qt = q_ref[...].astype(F32).T.astype(BF16)
    blk = lax.broadcasted_iota(jnp.int32, (nbp, L), 0)
    gate = jnp.dot(kmean_sc[...].astype(BF16), qt, preferred_element_type=F32)
    gate = jnp.where(blk < i, gate, NEG)
    sel = jnp.zeros((nbp, L), F32)
    for _ in range(MOBA_TOPK):
        m = jnp.max(gate, axis=0, keepdims=True)
        first = jnp.min(jnp.where(gate == m, blk, nbp), axis=0, keepdims=True)
        hit = blk == first
        sel = jnp.where(hit, jnp.where(m > 0.5 * NEG, 1.0, 0.0), sel)
        gate = jnp.where(hit, -jnp.inf, gate)
    sel_sc[...] = sel

    def scores(b0, n):
        k = k_ref[pl.ds(pl.multiple_of(b0 * L, L), n * L), :]
        return jnp.dot(k, qt, preferred_element_type=F32) + bias_sc[0:n * L, :]

    key = lax.broadcasted_iota(jnp.int32, (L, L), 0)
    qry = lax.broadcasted_iota(jnp.int32, (L, L), 1)
    s = jnp.where(key <= qry, scores(i, 1), NEG)
    carry = _online_update_t(_attn_init(L, MOBA_HEAD_DIM), [s], [vt_ref[0, 0, i]],
                             jnp.zeros((1, L), F32))

    def trip(g, carry):
        tiles = [scores(g * group + u * sub, sub) for u in range(group // sub)]
        for u in range(group // sub):
            b0 = g * group + u * sub
            s = tiles[u]
            parts = [jnp.where(sel_sc[pl.ds(b0 + jj, 1), :] > 0.5, s[jj * L:(jj + 1) * L, :], NEG)
                     for jj in range(sub)]
            vts = [vt_ref[0, 0, b0 + jj] for jj in range(sub)]
            carry = _online_update_t(carry, parts, vts, slope * ((b0 - i) * L).astype(F32))
        return carry

    _, l, acc = lax.fori_loop(0, (i + group - 1) // group, trip, carry)
    o_ref[...] = (acc / l).T.astype(o_ref.dtype)


def _moba_attn(qkv, vt, slopes, bsz, seq, group, sub):
    L = MOBA_BLOCK
    nq = seq // L
    hh = MOBA_HEADS
    nbp = -(-nq // 16) * 16
    assert nq % group == 0 and group % sub == 0
    return pl.pallas_call(
        functools.partial(_moba_kernel, n_blocks=nq, group=group, sub=sub),
        out_shape=jax.ShapeDtypeStruct((bsz * seq, MOBA_WIDTH), BF16),
        grid=(bsz, hh, nq),
        in_specs=[pl.BlockSpec((L, MOBA_HEAD_DIM), lambda b, h, i: (b * nq + i, h)),
                  pl.BlockSpec((seq, MOBA_HEAD_DIM), lambda b, h, i: (b, hh + h)),
                  pl.BlockSpec((1, 1, nq, MOBA_HEAD_DIM, L), lambda b, h, i: (b, h, 0, 0, 0)),
                  pl.BlockSpec((1, 1, L), lambda b, h, i: (h, 0, 0))],
        out_specs=pl.BlockSpec((L, MOBA_HEAD_DIM), lambda b, h, i: (b * nq + i, h)),
        scratch_shapes=[pltpu.VMEM((nbp, MOBA_HEAD_DIM), F32),
                        pltpu.VMEM((nbp, L), F32),
                        pltpu.VMEM((sub * L, L), F32)],
        compiler_params=_cparams(3),
        name="moba_attention",
    )(qkv, qkv, vt, slopes)


def _outproj_kernel(oa_ref, ob_ref, ga_ref, gb_ref, w_ref, gpost_ref, x_ref, mod_ref, gffn_ref,
                    x1_ref, h2_ref, h2t_ref):
    half = oa_ref.shape[1]
    a = (_rms(oa_ref[...].astype(F32)) * ga_ref[...]).astype(BF16)
    b = (_rms(ob_ref[...].astype(F32)) * gb_ref[...]).astype(BF16)
    y = jnp.dot(a, w_ref[:half, :], preferred_element_type=F32)
    y = y + jnp.dot(b, w_ref[half:, :], preferred_element_type=F32)
    x1 = x_ref[...] + mod_ref[0, 2:3, :] * (_rms(y) * gpost_ref[...])
    x1_ref[...] = x1
    h2 = _rms(x1) * gffn_ref[...] * (1.0 + mod_ref[0, 4:5, :]) + mod_ref[0, 3:4, :]
    h2_ref[...] = h2.astype(BF16)
    h2t_ref[...] = h2.T.astype(BF16)


def _outproj(oa, ob, ga, gb, w, gpost, x2, mod, gffn, seq, tm):
    t, d = x2.shape
    half = oa.shape[1]
    per_b = seq // tm
    row = lambda n: pl.BlockSpec((1, n), lambda i: (0, 0))
    return pl.pallas_call(
        _outproj_kernel,
        out_shape=(jax.ShapeDtypeStruct((t, d), F32), jax.ShapeDtypeStruct((t, d), BF16),
                   jax.ShapeDtypeStruct((d, t), BF16)),
        grid=(t // tm,),
        in_specs=[pl.BlockSpec((tm, half), lambda i: (i, 0)),
                  pl.BlockSpec((tm, half), lambda i: (i, 0)),
                  row(half), row(half),
                  pl.BlockSpec((2 * half, d), lambda i: (0, 0)),
                  row(d),
                  pl.BlockSpec((tm, d), lambda i: (i, 0)),
                  pl.BlockSpec((1, 6, d), lambda i: (i // per_b, 0, 0)),
                  row(d)],
        out_specs=(pl.BlockSpec((tm, d), lambda i: (i, 0)),
                   pl.BlockSpec((tm, d), lambda i: (i, 0)),
                   pl.BlockSpec((d, tm), lambda i: (0, i))),
        compiler_params=_cparams(1),
        name="out_proj",
    )(oa, ob, ga.reshape(1, half), gb.reshape(1, half), w, gpost.reshape(1, d), x2, mod,
      gffn.reshape(1, d))


def _qk(q, k):
    return lax.dot_general(q, k, (((1,), (1,)), ((), ())), preferred_element_type=F32)


def _peer_select_kernel(q_ref, k1_ref, k2_ref, s1_ref, s2_ref, st_ref):
    tm = q_ref.shape[0]
    hh, kk = PEER_HEADS, PEER_TOPK
    rowi = lax.broadcasted_iota(jnp.int32, (PEER_NKEYS, tm), 0)
    tops = ([], [])
    for h in range(hh):
        for p, (kref, sref) in enumerate(((k1_ref, s1_ref), (k2_ref, s2_ref))):
            c0 = (2 * h + p) * (PEER_DKEY // 2)
            s = _qk(kref[h], q_ref[:, c0:c0 + PEER_DKEY // 2])
            sref[h] = s
            vals = []
            for _ in range(kk):
                m = jnp.max(s, axis=0, keepdims=True)
                first = jnp.min(jnp.where(s == m, rowi, PEER_NKEYS), axis=0, keepdims=True)
                s = jnp.where(rowi == first, -jnp.inf, s)
                vals.append(m)
            tops[p].append(vals)
    v1 = [jnp.concatenate([tops[0][h][a] for h in range(hh)], axis=0) for a in range(kk)]
    v2 = [jnp.concatenate([tops[1][h][a] for h in range(hh)], axis=0) for a in range(kk)]
    cands = [v1[a] + v2[b] for a in range(kk) for b in range(kk) if (a + 1) * (b + 1) <= kk]
    best = v1[0] + v2[0]
    z = jnp.zeros_like(best)
    cur = best
    for r in range(kk):
        cur = functools.reduce(jnp.maximum, cands)
        z = z + jnp.exp(cur - best)
        if r + 1 < kk:
            todo = jnp.ones_like(best)
            nxt = []
            for c in cands:
                hit = jnp.where(c == cur, todo, 0.0)
                nxt.append(jnp.where(hit > 0.5, -jnp.inf, c))
                todo = todo - hit
            cands = nxt
    st_ref[0 * hh:1 * hh, :] = cur
    st_ref[1 * hh:2 * hh, :] = v1[0]
    st_ref[2 * hh:3 * hh, :] = v2[0]
    st_ref[3 * hh:4 * hh, :] = 1.0 / z


def _peer_select(q, k1, k2, tm):
    t, d = q.shape
    hh, nk = PEER_HEADS, PEER_NKEYS
    keys = pl.BlockSpec((hh, nk, PEER_DKEY // 2), lambda i: (0, 0, 0))
    sc = pl.BlockSpec((hh, nk, tm), lambda i: (0, 0, i))
    return pl.pallas_call(
        _peer_select_kernel,
        out_shape=(jax.ShapeDtypeStruct((hh, nk, t), F32), jax.ShapeDtypeStruct((hh, nk, t), F32),
                   jax.ShapeDtypeStruct((4 * hh, t), F32)),
        grid=(t // tm,),
        in_specs=[pl.BlockSpec((tm, d), lambda i: (i, 0)), keys, keys],
        out_specs=(sc, sc, pl.BlockSpec((4 * hh, tm), lambda i: (0, i))),
        compiler_params=_cparams(1),
        name="peer_select",
    )(q, k1, k2)


def _peer_main_kernel(h2t_ref, u_ref, vt_ref, s1_ref, s2_ref, st_ref, x1_ref, mod_ref, g_ref,
                      o_ref, acc_sc, e2_sc, c1_sc, p_sc, *, eb):
    hh, nk = PEER_HEADS, PEER_NKEYS
    e = pl.program_id(1)
    groups = eb // nk

    @pl.when(e == 0)
    def _():
        acc_sc[...] = jnp.zeros_like(acc_sc)
        for h in range(hh):
            e2_sc[h] = jnp.exp(s2_ref[h] - st_ref[2 * hh + h:2 * hh + h + 1, :])
            c1_sc[h] = (jnp.exp(s1_ref[h] - st_ref[hh + h:hh + h + 1, :])
                        * st_ref[3 * hh + h:3 * hh + h + 1, :])

    a = jnp.dot(u_ref[...], h2t_ref[...], preferred_element_type=F32)
    for g in range(groups):
        i1 = e * groups + g
        w = jnp.zeros((nk, a.shape[1]), F32)
        for h in range(hh):
            pair = s2_ref[h] + s1_ref[h, pl.ds(i1, 1), :]
            w = w + jnp.where(pair >= st_ref[h:h + 1, :], e2_sc[h], 0.0) * c1_sc[h, pl.ds(i1, 1), :]
        ag = a[g * nk:(g + 1) * nk, :]
        gelu = 0.5 * ag * (1.0 + lax.erf(ag * (1.0 / math.sqrt(2.0))))
        p_sc[g * nk:(g + 1) * nk, :] = (gelu * w).astype(BF16)
    acc_sc[...] += jnp.dot(vt_ref[...], p_sc[...], preferred_element_type=F32)

    @pl.when(e == pl.num_programs(1) - 1)
    def _():
        y = acc_sc[...].T
        o_ref[...] = x1_ref[...] + mod_ref[0, 5:6, :] * (_rms(y) * g_ref[...])


def _peer_main(h2t, u, vt, s1t, s2t, st, x1, mod, g, seq, tm, eb):
    d, t = h2t.shape
    hh, nk = PEER_HEADS, PEER_NKEYS
    per_b = seq // tm
    sc = pl.BlockSpec((hh, nk, tm), lambda i, e: (0, 0, i))
    return pl.pallas_call(
        functools.partial(_peer_main_kernel, eb=eb),
        out_shape=jax.ShapeDtypeStruct((t, d), F32),
        grid=(t // tm, PEER_EXPERTS // eb),
        in_specs=[pl.BlockSpec((d, tm), lambda i, e: (0, i)),
                  pl.BlockSpec((eb, d), lambda i, e: (e, 0)),
                  pl.BlockSpec((d, eb), lambda i, e: (0, e)),
                  sc, sc,
                  pl.BlockSpec((4 * hh, tm), lambda i, e: (0, i)),
                  pl.BlockSpec((tm, d), lambda i, e: (i, 0)),
                  pl.BlockSpec((1, 6, d), lambda i, e: (i // per_b, 0, 0)),
                  pl.BlockSpec((1, d), lambda i, e: (0, 0))],
        out_specs=pl.BlockSpec((tm, d), lambda i, e: (i, 0)),
        scratch_shapes=[pltpu.VMEM((d, tm), F32),
                        pltpu.VMEM((hh, nk, tm), F32),
                        pltpu.VMEM((hh, nk, tm), F32),
                        pltpu.VMEM((eb, tm), BF16)],
        compiler_params=_cparams(2),
        name="peer_experts",
    )(h2t, u, vt, s1t, s2t, st, x1, mod, g.reshape(1, d))


def _rope_tables(seq):
    half = MLA_ROPE // 2
    inv_freq = ROPE_THETA ** (-jnp.arange(half, dtype=F32) / half)
    ang = jnp.arange(seq, dtype=F32)[:, None] * inv_freq[None, :]
    zeros = jnp.zeros((seq, LANES - MLA_ROPE), F32)
    cc = jnp.concatenate([jnp.cos(ang), jnp.cos(ang), zeros], axis=-1)
    ss = jnp.concatenate([jnp.sin(ang), jnp.sin(ang), zeros], axis=-1)
    return cc, ss


def _rot_cols(w):
    half = MLA_ROPE // 2
    return jnp.concatenate([-w[..., half:], w[..., :half]], axis=-1)


def _layer(x2, mod, bsz, seq, w_in, g_pre_mix, g_q_lat, w_uq, g_kv_lat, w_ukv, g_out_mla,
           g_out_moba, w_out, g_post_mix, g_pre_ffn, w_peer_q, sub_keys_1, sub_keys_2,
           u_experts, v_experts, g_post_ffn):
    t = bsz * seq
    tm = min(512, seq)

    o_q, o_kv, o_kr, o_mq, o_mk, o_mv = 0, 768, 1280, 1344, 2368, 3392
    w_ql = w_in[:, o_q:o_kv].astype(BF16)
    w_kr = w_in[:, o_kr:o_mq]
    w_kvl = jnp.concatenate([w_in[:, o_kv:o_kr], w_kr, _rot_cols(w_kr)], axis=-1).astype(BF16)
    moba_scale = LOG2E / math.sqrt(MOBA_HEAD_DIM)
    w_m = jnp.concatenate([w_in[:, o_mq:o_mk] * moba_scale, w_in[:, o_mk:]], axis=-1).astype(BF16)
    wq3 = w_uq.reshape(MLA_Q_RANK, MLA_HEADS, MLA_QK) * (LOG2E / math.sqrt(MLA_QK))
    wq_rope = wq3[..., MLA_NOPE:]
    w_uq_p = jnp.concatenate([wq3[..., :MLA_NOPE], wq_rope, _rot_cols(wq_rope)], axis=-1)
    w_uq_p = w_uq_p.reshape(MLA_Q_RANK, MLA_HEADS * MLA_HEAD_PAD).astype(BF16)
    wkv3 = w_ukv.reshape(MLA_KV_RANK, MLA_HEADS, MLA_NOPE + MLA_V)
    w_ukv_p = jnp.concatenate([wkv3[..., :MLA_NOPE].reshape(MLA_KV_RANK, -1),
                               wkv3[..., MLA_NOPE:].reshape(MLA_KV_RANK, -1)], axis=-1).astype(BF16)
    cc, ss = _rope_tables(seq)
    slopes = LOG2E * 2.0 ** (-8.0 * (jnp.arange(MOBA_HEADS, dtype=F32) + 1.0) / MOBA_HEADS)
    slopes = jnp.broadcast_to(slopes[:, None, None], (MOBA_HEADS, 1, MOBA_BLOCK))

    h = _prenorm(x2, g_pre_mix, mod, seq, tm)
    q = _mla_q(h, w_ql, g_q_lat, w_uq_p, cc, ss, seq, tm)
    k, vt = _mla_kv(h, w_kvl, g_kv_lat, w_ukv_p, cc, ss, bsz, seq, tm)
    qkv_m = _matmul(h, w_m, min(1024, t), 1024, "moba_qkv_proj")
    o_mla = _mla_attn(q, k, vt, bsz, seq, min(1024, seq))
    n_blk = seq // MOBA_BLOCK
    vt_m = qkv_m[:, 2 * MOBA_WIDTH:].reshape(bsz, n_blk, MOBA_BLOCK, MOBA_HEADS, MOBA_HEAD_DIM)
    vt_m = vt_m.transpose(0, 3, 1, 4, 2)
    o_moba = _moba_attn(qkv_m, vt_m, slopes, bsz, seq, MOBA_GROUP, MOBA_SUB)
    x1, h2, h2t = _outproj(o_mla, o_moba, g_out_mla, g_out_moba, w_out.astype(BF16), g_post_mix,
                           x2, mod, g_pre_ffn, seq, tm)

    pq = _matmul(h2, w_peer_q.astype(BF16), min(1024, t), 1024, "peer_q_proj")
    s1t, s2t, st = _peer_select(pq, sub_keys_1.astype(BF16), sub_keys_2.astype(BF16), min(256, t))
    return _peer_main(h2t, u_experts.astype(BF16), v_experts.T.astype(BF16), s1t, s2t, st, x1, mod,
                      g_post_ffn, seq, tm, 512)


def kernel(x, c, w_ada, b_ada, g_pre_mix, w_in, g_q_lat, w_uq, g_kv_lat, w_ukv, g_out_mla,
           g_out_moba, w_out, g_post_mix, g_pre_ffn, w_peer_q, sub_keys_1, sub_keys_2, u_experts,
           v_experts, g_post_ffn):
    bsz, seq, d = x.shape
    x2 = x.reshape(bsz * seq, d)
    for l in range(w_ada.shape[0]):
        mod = _ada(c, w_ada[l], b_ada[l])
        x2 = _layer(x2, mod, bsz, seq, w_in[l], g_pre_mix[l], g_q_lat[l], w_uq[l], g_kv_lat[l],
                    w_ukv[l], g_out_mla[l], g_out_moba[l], w_out[l], g_post_mix[l], g_pre_ffn[l],
                    w_peer_q[l], sub_keys_1[l], sub_keys_2[l], u_experts[l], v_experts[l],
                    g_post_ffn[l])
    return x2.reshape(bsz, seq, d)
```

```python
import functools
import math

import jax
import jax.numpy as jnp
from jax import lax
from jax.experimental import pallas as pl
from jax.experimental.pallas import tpu as pltpu

F32 = jnp.float32
BF16 = jnp.bfloat16

D_MODEL = 2048
MLA_HEADS = 8
MLA_NOPE = 128
MLA_ROPE = 64
MLA_QK = MLA_NOPE + MLA_ROPE
MLA_V = 128
MLA_Q_RANK = 768
MLA_KV_RANK = 512
MLA_HEAD_PAD = 256
ROPE_THETA = 10000.0
MOBA_HEADS = 8
MOBA_HEAD_DIM = 128
MOBA_WIDTH = MOBA_HEADS * MOBA_HEAD_DIM
MOBA_BLOCK = 256
MOBA_TOPK = 3
PEER_HEADS = 8
PEER_NKEYS = 128
PEER_EXPERTS = PEER_NKEYS * PEER_NKEYS
PEER_DKEY = 256
PEER_TOPK = 16
EPS = 1e-6
NEG = -1e30
LOG2E = 1.4426950408889634
MOBA_GROUP = 8
MOBA_SUB = 2
KEY_CHUNK = 256
LANES = 128
VMEM_LIMIT = 56 * 1024 * 1024


def _cparams(n_axes, vmem=VMEM_LIMIT):
    return pltpu.CompilerParams(dimension_semantics=("arbitrary",) * n_axes,
                                vmem_limit_bytes=vmem)


def _rms(x):
    return x * lax.rsqrt(jnp.mean(x * x, axis=-1, keepdims=True) + EPS)


def _ada_kernel(c_ref, w_ref, b_ref, o_ref):
    c = c_ref[...]
    cs = c / (1.0 + jnp.exp(-c))
    o_ref[...] = jnp.dot(cs, w_ref[...], preferred_element_type=F32) + b_ref[...]


def _ada(c, w, b):
    bsz, d = c.shape
    n = w.shape[1]
    rows = 8
    cp = jnp.zeros((rows, d), F32).at[:bsz].set(c)
    tn = 1024
    out = pl.pallas_call(
        _ada_kernel,
        out_shape=jax.ShapeDtypeStruct((rows, n), F32),
        grid=(n // tn,),
        in_specs=[pl.BlockSpec((rows, d), lambda j: (0, 0)),
                  pl.BlockSpec((d, tn), lambda j: (0, j)),
                  pl.BlockSpec((1, tn), lambda j: (0, j))],
        out_specs=pl.BlockSpec((rows, tn), lambda j: (0, j)),
        compiler_params=_cparams(1),
        name="ada_mod",
    )(cp, w, b.reshape(1, n))
    return out[:bsz].reshape(bsz, 6, d)


def _prenorm_kernel(x_ref, g_ref, mod_ref, h_ref):
    y = _rms(x_ref[...]) * g_ref[...]
    h_ref[...] = (y * (1.0 + mod_ref[0, 1:2, :]) + mod_ref[0, 0:1, :]).astype(h_ref.dtype)


def _prenorm(x2, g, mod, seq, tm):
    t, d = x2.shape
    per_b = seq // tm
    return pl.pallas_call(
        _prenorm_kernel,
        out_shape=jax.ShapeDtypeStruct((t, d), BF16),
        grid=(t // tm,),
        in_specs=[pl.BlockSpec((tm, d), lambda i: (i, 0)),
                  pl.BlockSpec((1, d), lambda i: (0, 0)),
                  pl.BlockSpec((1, 6, d), lambda i: (i // per_b, 0, 0))],
        out_specs=pl.BlockSpec((tm, d), lambda i: (i, 0)),
        compiler_params=_cparams(1),
        name="prenorm_mod",
    )(x2, g.reshape(1, d), mod)


def _mm_kernel(a_ref, w_ref, o_ref):
    o_ref[...] = jnp.dot(a_ref[...], w_ref[...], preferred_element_type=F32).astype(o_ref.dtype)


def _matmul(a, w, tm, tn, name):
    t, k = a.shape
    n = w.shape[1]
    return pl.pallas_call(
        _mm_kernel,
        out_shape=jax.ShapeDtypeStruct((t, n), BF16),
        grid=(t // tm, n // tn),
        in_specs=[pl.BlockSpec((tm, k), lambda i, j: (i, 0)),
                  pl.BlockSpec((k, tn), lambda i, j: (0, j))],
        out_specs=pl.BlockSpec((tm, tn), lambda i, j: (i, j)),
        compiler_params=_cparams(2),
        name=name,
    )(a, w)


def _rope_half(r, cc, ss):
    return r * cc + pltpu.roll(r, 64, 1) * ss


def _mla_q_kernel(h_ref, wl_ref, g_ref, wu_ref, cc_ref, ss_ref, q_ref):
    ql = jnp.dot(h_ref[...], wl_ref[...], preferred_element_type=F32)
    qn = (_rms(ql) * g_ref[...]).astype(BF16)
    q2 = jnp.dot(qn, wu_ref[...], preferred_element_type=F32)
    cc = cc_ref[...]
    ss = ss_ref[...]
    for h in range(MLA_HEADS):
        base = h * MLA_HEAD_PAD
        q_ref[:, base:base + MLA_NOPE] = q2[:, base:base + MLA_NOPE].astype(BF16)
        r = q2[:, base + MLA_NOPE:base + MLA_HEAD_PAD]
        q_ref[:, base + MLA_NOPE:base + MLA_HEAD_PAD] = _rope_half(r, cc, ss).astype(BF16)


def _mla_q(h, wl, g, wu, cc, ss, seq, tm):
    t, d = h.shape
    per_b = seq // tm
    n = MLA_HEADS * MLA_HEAD_PAD
    return pl.pallas_call(
        _mla_q_kernel,
        out_shape=jax.ShapeDtypeStruct((t, n), BF16),
        grid=(t // tm,),
        in_specs=[pl.BlockSpec((tm, d), lambda i: (i, 0)),
                  pl.BlockSpec((d, MLA_Q_RANK), lambda i: (0, 0)),
                  pl.BlockSpec((1, MLA_Q_RANK), lambda i: (0, 0)),
                  pl.BlockSpec((MLA_Q_RANK, n), lambda i: (0, 0)),
                  pl.BlockSpec((tm, LANES), lambda i: (i % per_b, 0)),
                  pl.BlockSpec((tm, LANES), lambda i: (i % per_b, 0))],
        out_specs=pl.BlockSpec((tm, n), lambda i: (i, 0)),
        compiler_params=_cparams(1),
        name="mla_q_path",
    )(h, wl, g.reshape(1, MLA_Q_RANK), wu, cc, ss)


def _mla_kv_kernel(h_ref, wl_ref, g_ref, wu_ref, cc_ref, ss_ref, k_ref, vt_ref):
    kl = jnp.dot(h_ref[...], wl_ref[...], preferred_element_type=F32)
    kvn = (_rms(kl[:, :MLA_KV_RANK]) * g_ref[...]).astype(BF16)
    kr = _rope_half(kl[:, MLA_KV_RANK:], cc_ref[...], ss_ref[...]).astype(BF16)
    kv2 = jnp.dot(kvn, wu_ref[...], preferred_element_type=F32)
    for h in range(MLA_HEADS):
        base = h * MLA_HEAD_PAD
        k_ref[:, base:base + MLA_NOPE] = kv2[:, h * MLA_NOPE:(h + 1) * MLA_NOPE].astype(BF16)
        k_ref[:, base + MLA_NOPE:base + MLA_HEAD_PAD] = kr
    for c in range(vt_ref.shape[2]):
        v = kv2[c * KEY_CHUNK:(c + 1) * KEY_CHUNK, MLA_HEADS * MLA_NOPE:]
        vt_ref[0, :, c] = v.T.reshape(MLA_HEADS, MLA_V, KEY_CHUNK).astype(BF16)


def _mla_kv(h, wl, g, wu, cc, ss, bsz, seq, tm):
    t, d = h.shape
    per_b = seq // tm
    cpt = tm // KEY_CHUNK
    nl = MLA_KV_RANK + LANES
    nk = MLA_HEADS * MLA_HEAD_PAD
    vt_shape = (bsz, MLA_HEADS, seq // KEY_CHUNK, MLA_V, KEY_CHUNK)
    return pl.pallas_call(
        _mla_kv_kernel,
        out_shape=(jax.ShapeDtypeStruct((t, nk), BF16), jax.ShapeDtypeStruct(vt_shape, BF16)),
        grid=(t // tm,),
        in_specs=[pl.BlockSpec((tm, d), lambda i: (i, 0)),
                  pl.BlockSpec((d, nl), lambda i: (0, 0)),
                  pl.BlockSpec((1, MLA_KV_RANK), lambda i: (0, 0)),
                  pl.BlockSpec((MLA_KV_RANK, nk), lambda i: (0, 0)),
                  pl.BlockSpec((tm, LANES), lambda i: (i % per_b, 0)),
                  pl.BlockSpec((tm, LANES), lambda i: (i % per_b, 0))],
        out_specs=(pl.BlockSpec((tm, nk), lambda i: (i, 0)),
                   pl.BlockSpec((1, MLA_HEADS, cpt, MLA_V, KEY_CHUNK),
                                lambda i: (i // per_b, 0, i % per_b, 0, 0))),
        compiler_params=_cparams(1),
        name="mla_kv_path",
    )(h, wl, g.reshape(1, MLA_KV_RANK), wu, cc, ss)


def _online_update_t(carry, parts, vts, offset):
    m, l, acc = carry
    mb = functools.reduce(jnp.maximum, [jnp.max(x, axis=0, keepdims=True) for x in parts])
    m_new = jnp.maximum(m, mb + offset)
    alpha = jnp.exp2(m - m_new)
    shift = offset - m_new
    l = alpha * l
    acc = alpha * acc
    for x, vt in zip(parts, vts):
        p = jnp.exp2(x + shift)
        l = l + jnp.sum(p, axis=0, keepdims=True)
        acc = acc + jnp.dot(vt, p.astype(BF16), preferred_element_type=F32)
    return m_new, l, acc


def _attn_init(tq, dv):
    return (jnp.full((1, tq), NEG, F32), jnp.zeros((1, tq), F32), jnp.zeros((dv, tq), F32))


def _mla_attn_kernel(q_ref, k_ref, vt_ref, o_ref, *, tq):
    C = KEY_CHUNK
    per = tq // C
    i = pl.program_id(2)
    qt = q_ref[...].astype(F32).T.astype(BF16)
    zero = jnp.zeros((1, tq), F32)

    def scores(c):
        k = k_ref[pl.ds(pl.multiple_of(c * C, C), C), :]
        return jnp.dot(k, qt, preferred_element_type=F32)

    def trip(t, carry):
        tiles = [scores(t * per + u) for u in range(per)]
        for u in range(per):
            carry = _online_update_t(carry, [tiles[u]], [vt_ref[0, 0, t * per + u]], zero)
        return carry

    carry = lax.fori_loop(0, i, trip, _attn_init(tq, MLA_V))
    key = lax.broadcasted_iota(jnp.int32, (C, tq), 0)
    qry = lax.broadcasted_iota(jnp.int32, (C, tq), 1)
    for u in range(per):
        c = i * per + u
        s = jnp.where(key + u * C <= qry, scores(c), NEG)
        carry = _online_update_t(carry, [s], [vt_ref[0, 0, c]], zero)
    _, l, acc = carry
    o_ref[...] = (acc / l).T.astype(o_ref.dtype)


def _mla_attn(q, k, vt, bsz, seq, tq):
    nq = seq // tq
    nc = seq // KEY_CHUNK
    return pl.pallas_call(
        functools.partial(_mla_attn_kernel, tq=tq),
        out_shape=jax.ShapeDtypeStruct((bsz * seq, MLA_HEADS * MLA_V), BF16),
        grid=(bsz, MLA_HEADS, nq),
        in_specs=[pl.BlockSpec((tq, MLA_HEAD_PAD), lambda b, h, i: (b * nq + i, h)),
                  pl.BlockSpec((seq, MLA_HEAD_PAD), lambda b, h, i: (b, h)),
                  pl.BlockSpec((1, 1, nc, MLA_V, KEY_CHUNK), lambda b, h, i: (b, h, 0, 0, 0))],
        out_specs=pl.BlockSpec((tq, MLA_V), lambda b, h, i: (b * nq + i, h)),
        compiler_params=_cparams(3),
        name="mla_attention",
    )(q, k, vt)


def _moba_kernel(q_ref, k_ref, vt_ref, slope_ref, o_ref, kmean_sc, sel_sc, bias_sc, *,
                 n_blocks, group, sub):
    L = MOBA_BLOCK
    nbp = kmean_sc.shape[0]
    i = pl.program_id(2)
    slope = slope_ref[0]

    @pl.when(i == 0)
    def _():
        kmean_sc[...] = jnp.zeros_like(kmean_sc)
        for n in range(n_blocks):
            kb = k_ref[n * L:(n + 1) * L, :].astype(F32)
            kmean_sc[n:n + 1, :] = jnp.sum(kb, axis=0, keepdims=True) * (1.0 / L)
        bias_sc[...] = slope * lax.broadcasted_iota(jnp.int32, bias_sc.shape, 0).astype(F32)

    qt = q_ref[...].astype(F32).T.astype(BF16)
    blk = lax.broadcasted_iota(jnp.int32, (nbp, L), 0)
    gate = jnp.dot(kmean_sc[...].astype(BF16), qt, preferred_element_type=F32)
    gate = jnp.where(blk < i, gate, NEG)
    sel = jnp.zeros((nbp, L), F32)
    for _ in range(MOBA_TOPK):
        m = jnp.max(gate, axis=0, keepdims=True)
        first = jnp.min(jnp.where(gate == m, blk, nbp), axis=0, keepdims=True)
        hit = blk == first
        sel = jnp.where(hit, jnp.where(m > 0.5 * NEG, 1.0, 0.0), sel)
        gate = jnp.where(hit, -jnp.inf, gate)
    sel_sc[...] = sel

    def scores(b0, n):
        k = k_ref[pl.ds(pl.multiple_of(b0 * L, L), n * L), :]
        return jnp.dot(k, qt, preferred_element_type=F32) + bias_sc[0:n * L, :]

    key = lax.broadcasted_iota(jnp.int32, (L, L), 0)
    qry = lax.broadcasted_iota(jnp.int32, (L, L), 1)
    s = jnp.where(key <= qry, scores(i, 1), NEG)
    carry = _online_update_t(_attn_init(L, MOBA_HEAD_DIM), [s], [vt_ref[0, 0, i]],
                             jnp.zeros((1, L), F32))

    def trip(g, carry):
        tiles = [scores(g * group + u * sub, sub) for u in range(group // sub)]
        for u in range(group // sub):
            b0 = g * group + u * sub
            s = tiles[u]
            parts = [jnp.where(sel_sc[pl.ds(b0 + jj, 1), :] > 0.5, s[jj * L:(jj + 1) * L, :], NEG)
                     for jj in range(sub)]
            vts = [vt_ref[0, 0, b0 + jj] for jj in range(sub)]
            carry = _online_update_t(carry, parts, vts, slope * ((b0 - i) * L).astype(F32))
        return carry

    _, l, acc = lax.fori_loop(0, (i + group - 1) // group, trip, carry)
    o_ref[...] = (acc / l).T.astype(o_ref.dtype)


def _moba_attn(qkv, vt, slopes, bsz, seq, group, sub):
    L = MOBA_BLOCK
    nq = seq // L
    hh = MOBA_HEADS
    nbp = -(-nq // 16) * 16
    assert nq % group == 0 and group % sub == 0
    return pl.pallas_call(
        functools.partial(_moba_kernel, n_blocks=nq, group=group, sub=sub),
        out_shape=jax.ShapeDtypeStruct((bsz * seq, MOBA_WIDTH), BF16),
        grid=(bsz, hh, nq),
        in_specs=[pl.BlockSpec((L, MOBA_HEAD_DIM), lambda b, h, i: (b * nq + i, h)),
                  pl.BlockSpec((seq, MOBA_HEAD_DIM), lambda b, h, i: (b, hh + h)),
                  pl.BlockSpec((1, 1, nq, MOBA_HEAD_DIM, L), lambda b, h, i: (b, h, 0, 0, 0)),
                  pl.BlockSpec((1, 1, L), lambda b, h, i: (h, 0, 0))],
        out_specs=pl.BlockSpec((L, MOBA_HEAD_DIM), lambda b, h, i: (b * nq + i, h)),
        scratch_shapes=[pltpu.VMEM((nbp, MOBA_HEAD_DIM), F32),
                        pltpu.VMEM((nbp, L), F32),
                        pltpu.VMEM((sub * L, L), F32)],
        compiler_params=_cparams(3),
        name="moba_attention",
    )(qkv, qkv, vt, slopes)


def _outproj_kernel(oa_ref, ob_ref, ga_ref, gb_ref, w_ref, gpost_ref, x_ref, mod_ref, gffn_ref,
                    x1_ref, h2_ref, h2t_ref):
    half = oa_ref.shape[1]
    a = (_rms(oa_ref[...].astype(F32)) * ga_ref[...]).astype(BF16)
    b = (_rms(ob_ref[...].astype(F32)) * gb_ref[...]).astype(BF16)
    y = jnp.dot(a, w_ref[:half, :], preferred_element_type=F32)
    y = y + jnp.dot(b, w_ref[half:, :], preferred_element_type=F32)
    x1 = x_ref[...] + mod_ref[0, 2:3, :] * (_rms(y) * gpost_ref[...])
    x1_ref[...] = x1
    h2 = _rms(x1) * gffn_ref[...] * (1.0 + mod_ref[0, 4:5, :]) + mod_ref[0, 3:4, :]
    h2_ref[...] = h2.astype(BF16)
    h2t_ref[...] = h2.T.astype(BF16)


def _outproj(oa, ob, ga, gb, w, gpost, x2, mod, gffn, seq, tm):
    t, d = x2.shape
    half = oa.shape[1]
    per_b = seq // tm
    row = lambda n: pl.BlockSpec((1, n), lambda i: (0, 0))
    return pl.pallas_call(
        _outproj_kernel,
        out_shape=(jax.ShapeDtypeStruct((t, d), F32), jax.ShapeDtypeStruct((t, d), BF16),
                   jax.ShapeDtypeStruct((d, t), BF16)),
        grid=(t // tm,),
        in_specs=[pl.BlockSpec((tm, half), lambda i: (i, 0)),
                  pl.BlockSpec((tm, half), lambda i: (i, 0)),
                  row(half), row(half),
                  pl.BlockSpec((2 * half, d), lambda i: (0, 0)),
                  row(d),
                  pl.BlockSpec((tm, d), lambda i: (i, 0)),
                  pl.BlockSpec((1, 6, d), lambda i: (i // per_b, 0, 0)),
                  row(d)],
        out_specs=(pl.BlockSpec((tm, d), lambda i: (i, 0)),
                   pl.BlockSpec((tm, d), lambda i: (i, 0)),
                   pl.BlockSpec((d, tm), lambda i: (0, i))),
        compiler_params=_cparams(1),
        name="out_proj",
    )(oa, ob, ga.reshape(1, half), gb.reshape(1, half), w, gpost.reshape(1, d), x2, mod,
      gffn.reshape(1, d))


def _qk(q, k):
    return lax.dot_general(q, k, (((1,), (1,)), ((), ())), preferred_element_type=F32)


def _peer_select_kernel(q_ref, k1_ref, k2_ref, thr_ref, c1_ref, s2_ref, e2_ref, top_sc, s1_sc):
    tm = q_ref.shape[0]
    hh, kk, nk = PEER_HEADS, PEER_TOPK, PEER_NKEYS
    rowi = lax.broadcasted_iota(jnp.int32, (nk, tm), 0)

    def scores(h, p):
        c0 = (2 * h + p) * (PEER_DKEY // 2)
        kref = k1_ref if p == 0 else k2_ref
        return _qk(kref[h], q_ref[:, c0:c0 + PEER_DKEY // 2])

    clean = jnp.ones((1, tm), F32)
    for h in range(hh):
        for p in range(2):
            s = scores(h, p)
            if p == 0:
                s1_sc[h] = s
            else:
                s2_ref[h] = s
            for a in range(kk):
                m = jnp.max(s, axis=0, keepdims=True)
                top_sc[p, a, h:h + 1, :] = m
                s = jnp.where(s == m, -jnp.inf, s)
            gone = jnp.sum(jnp.where(s == -jnp.inf, 1.0, 0.0), axis=0, keepdims=True)
            clean = jnp.where(gone == float(kk), clean, 0.0)

    @pl.when(jnp.min(clean) < 0.5)
    def _():
        for h in range(hh):
            for p in range(2):
                s = s1_sc[h] if p == 0 else s2_ref[h]
                for a in range(kk):
                    m = jnp.max(s, axis=0, keepdims=True)
                    top_sc[p, a, h:h + 1, :] = m
                    first = jnp.min(jnp.where(s == m, rowi, nk), axis=0, keepdims=True)
                    s = jnp.where(rowi == first, -jnp.inf, s)

    v1 = [top_sc[0, a] for a in range(kk)]
    v2 = [top_sc[1, a] for a in range(kk)]
    cands = [v1[a] + v2[b] for a in range(kk) for b in range(kk) if (a + 1) * (b + 1) <= kk]
    best = v1[0] + v2[0]
    z = jnp.zeros_like(best)
    cur = best
    for r in range(kk):
        cur = functools.reduce(jnp.maximum, cands)
        z = z + jnp.exp(cur - best)
        if r + 1 < kk:
            todo = jnp.ones_like(best)
            nxt = []
            for c in cands:
                hit = jnp.where(c == cur, todo, 0.0)
                nxt.append(jnp.where(hit > 0.5, -jnp.inf, c))
                todo = todo - hit
            cands = nxt
    inv_z = 1.0 / z
    for h in range(hh):
        s1 = s1_sc[h]
        tau = cur[h:h + 1, :]
        thr = jnp.full((nk, tm), jnp.inf, F32)
        for b in range(kk):
            v = v2[b][h:h + 1, :]
            thr = jnp.where(s1 + v >= tau, v, thr)
        thr_ref[h] = thr
        c1_ref[h] = jnp.exp(s1 - v1[0][h:h + 1, :]) * inv_z[h:h + 1, :]
        e2_ref[h] = jnp.exp(s2_ref[h] - v2[0][h:h + 1, :])


def _peer_select(q, k1, k2, tm):
    t, d = q.shape
    hh, nk, kk = PEER_HEADS, PEER_NKEYS, PEER_TOPK
    keys = pl.BlockSpec((hh, nk, PEER_DKEY // 2), lambda i: (0, 0, 0))
    sc = pl.BlockSpec((hh, nk, tm), lambda i: (0, 0, i))
    table = jax.ShapeDtypeStruct((hh, nk, t), F32)
    return pl.pallas_call(
        _peer_select_kernel,
        out_shape=(table, table, table, table),
        grid=(t // tm,),
        in_specs=[pl.BlockSpec((tm, d), lambda i: (i, 0)), keys, keys],
        out_specs=(sc, sc, sc, sc),
        scratch_shapes=[pltpu.VMEM((2, kk, hh, tm), F32),
                        pltpu.VMEM((hh, nk, tm), F32)],
        compiler_params=_cparams(1),
        name="peer_select",
    )(q, k1, k2)


def _peer_main_kernel(h2t_ref, u_ref, vt_ref, thr_ref, c1_ref, s2_ref, e2_ref, x1_ref, mod_ref,
                      g_ref, o_ref, acc_sc, p_sc, *, eb):
    hh, nk = PEER_HEADS, PEER_NKEYS
    e = pl.program_id(1)
    groups = eb // nk

    @pl.when(e == 0)
    def _():
        acc_sc[...] = jnp.zeros_like(acc_sc)

    a = jnp.dot(u_ref[...], h2t_ref[...], preferred_element_type=F32)
    for g in range(groups):
        i1 = e * groups + g
        w = jnp.zeros((nk, a.shape[1]), F32)
        for h in range(hh):
            picked = s2_ref[h] >= thr_ref[h, pl.ds(i1, 1), :]
            w = w + jnp.where(picked, e2_ref[h], 0.0) * c1_ref[h, pl.ds(i1, 1), :]
        ag = a[g * nk:(g + 1) * nk, :]
        gelu = 0.5 * ag * (1.0 + lax.erf(ag * (1.0 / math.sqrt(2.0))))
        p_sc[g * nk:(g + 1) * nk, :] = (gelu * w).astype(BF16)
    acc_sc[...] += jnp.dot(vt_ref[...], p_sc[...], preferred_element_type=F32)

    @pl.when(e == pl.num_programs(1) - 1)
    def _():
        y = acc_sc[...].T
        o_ref[...] = x1_ref[...] + mod_ref[0, 5:6, :] * (_rms(y) * g_ref[...])


def _peer_main(h2t, u, vt, thr, c1, s2t, e2, x1, mod, g, seq, tm, eb):
    d, t = h2t.shape
    hh, nk = PEER_HEADS, PEER_NKEYS
    per_b = seq // tm
    once = pl.Buffered(1)
    sc = pl.BlockSpec((hh, nk, tm), lambda i, e: (0, 0, i), pipeline_mode=once)
    return pl.pallas_call(
        functools.partial(_peer_main_kernel, eb=eb),
        out_shape=jax.ShapeDtypeStruct((t, d), F32),
        grid=(t // tm, PEER_EXPERTS // eb),
        in_specs=[pl.BlockSpec((d, tm), lambda i, e: (0, i)),
                  pl.BlockSpec((eb, d), lambda i, e: (e, 0)),
                  pl.BlockSpec((d, eb), lambda i, e: (0, e)),
                  sc, sc, sc, sc,
                  pl.BlockSpec((tm, d), lambda i, e: (i, 0), pipeline_mode=once),
                  pl.BlockSpec((1, 6, d), lambda i, e: (i // per_b, 0, 0)),
                  pl.BlockSpec((1, d), lambda i, e: (0, 0))],
        out_specs=pl.BlockSpec((tm, d), lambda i, e: (i, 0)),
        scratch_shapes=[pltpu.VMEM((d, tm), F32),
                        pltpu.VMEM((eb, tm), BF16)],
        compiler_params=_cparams(2),
        name="peer_experts",
    )(h2t, u, vt, thr, c1, s2t, e2, x1, mod, g.reshape(1, d))


def _rope_tables(seq):
    half = MLA_ROPE // 2
    inv_freq = ROPE_THETA ** (-jnp.arange(half, dtype=F32) / half)
    ang = jnp.arange(seq, dtype=F32)[:, None] * inv_freq[None, :]
    zeros = jnp.zeros((seq, LANES - MLA_ROPE), F32)
    cc = jnp.concatenate([jnp.cos(ang), jnp.cos(ang), zeros], axis=-1)
    ss = jnp.concatenate([jnp.sin(ang), jnp.sin(ang), zeros], axis=-1)
    return cc, ss


def _rot_cols(w):
    half = MLA_ROPE // 2
    return jnp.concatenate([-w[..., half:], w[..., :half]], axis=-1)


def _layer(x2, mod, bsz, seq, w_in, g_pre_mix, g_q_lat, w_uq, g_kv_lat, w_ukv, g_out_mla,
           g_out_moba, w_out, g_post_mix, g_pre_ffn, w_peer_q, sub_keys_1, sub_keys_2,
           u_experts, v_experts, g_post_ffn):
    t = bsz * seq
    tm = min(512, seq)

    o_q, o_kv, o_kr, o_mq, o_mk, o_mv = 0, 768, 1280, 1344, 2368, 3392
    w_ql = w_in[:, o_q:o_kv].astype(BF16)
    w_kr = w_in[:, o_kr:o_mq]
    w_kvl = jnp.concatenate([w_in[:, o_kv:o_kr], w_kr, _rot_cols(w_kr)], axis=-1).astype(BF16)
    moba_scale = LOG2E / math.sqrt(MOBA_HEAD_DIM)
    w_m = jnp.concatenate([w_in[:, o_mq:o_mk] * moba_scale, w_in[:, o_mk:]], axis=-1).astype(BF16)
    wq3 = w_uq.reshape(MLA_Q_RANK, MLA_HEADS, MLA_QK) * (LOG2E / math.sqrt(MLA_QK))
    wq_rope = wq3[..., MLA_NOPE:]
    w_uq_p = jnp.concatenate([wq3[..., :MLA_NOPE], wq_rope, _rot_cols(wq_rope)], axis=-1)
    w_uq_p = w_uq_p.reshape(MLA_Q_RANK, MLA_HEADS * MLA_HEAD_PAD).astype(BF16)
    wkv3 = w_ukv.reshape(MLA_KV_RANK, MLA_HEADS, MLA_NOPE + MLA_V)
    w_ukv_p = jnp.concatenate([wkv3[..., :MLA_NOPE].reshape(MLA_KV_RANK, -1),
                               wkv3[..., MLA_NOPE:].reshape(MLA_KV_RANK, -1)], axis=-1).astype(BF16)
    cc, ss = _rope_tables(seq)
    slopes = LOG2E * 2.0 ** (-8.0 * (jnp.arange(MOBA_HEADS, dtype=F32) + 1.0) / MOBA_HEADS)
    slopes = jnp.broadcast_to(slopes[:, None, None], (MOBA_HEADS, 1, MOBA_BLOCK))

    h = _prenorm(x2, g_pre_mix, mod, seq, tm)
    q = _mla_q(h, w_ql, g_q_lat, w_uq_p, cc, ss, seq, tm)
    k, vt = _mla_kv(h, w_kvl, g_kv_lat, w_ukv_p, cc, ss, bsz, seq, tm)
    qkv_m = _matmul(h, w_m, min(1024, t), 1024, "moba_qkv_proj")
    o_mla = _mla_attn(q, k, vt, bsz, seq, min(1024, seq))
    n_blk = seq // MOBA_BLOCK
    vt_m = qkv_m[:, 2 * MOBA_WIDTH:].reshape(bsz, n_blk, MOBA_BLOCK, MOBA_HEADS, MOBA_HEAD_DIM)
    vt_m = vt_m.transpose(0, 3, 1, 4, 2)
    o_moba = _moba_attn(qkv_m, vt_m, slopes, bsz, seq, MOBA_GROUP, MOBA_SUB)
    x1, h2, h2t = _outproj(o_mla, o_moba, g_out_mla, g_out_moba, w_out.astype(BF16), g_post_mix,
                           x2, mod, g_pre_ffn, seq, tm)

    pq = _matmul(h2, w_peer_q.astype(BF16), min(1024, t), 1024, "peer_q_proj")
    thr, c1, s2t, e2 = _peer_select(pq, sub_keys_1.astype(BF16), sub_keys_2.astype(BF16),
                                    min(256, t))
    return _peer_main(h2t, u_experts.astype(BF16), v_experts.T.astype(BF16), thr, c1, s2t, e2, x1,
                      mod, g_post_ffn, seq, tm, 1024)


def kernel(x, c, w_ada, b_ada, g_pre_mix, w_in, g_q_lat, w_uq, g_kv_lat, w_ukv, g_out_mla,
           g_out_moba, w_out, g_post_mix, g_pre_ffn, w_peer_q, sub_keys_1, sub_keys_2, u_experts,
           v_experts, g_post_ffn):
    bsz, seq, d = x.shape
    x2 = x.reshape(bsz * seq, d)
    for l in range(w_ada.shape[0]):
        mod = _ada(c, w_ada[l], b_ada[l])
        x2 = _layer(x2, mod, bsz, seq, w_in[l], g_pre_mix[l], g_q_lat[l], w_uq[l], g_kv_lat[l],
                    w_ukv[l], g_out_mla[l], g_out_moba[l], w_out[l], g_post_mix[l], g_pre_ffn[l],
                    w_peer_q[l], sub_keys_1[l], sub_keys_2[l], u_experts[l], v_experts[l],
                    g_post_ffn[l])
    return x2.reshape(bsz, seq, d)
```

```python
import functools
import math

import jax
import jax.numpy as jnp
from jax import lax
from jax.experimental import pallas as pl
from jax.experimental.pallas import tpu as pltpu

F32 = jnp.float32
BF16 = jnp.bfloat16

D_MODEL = 2048
MLA_HEADS = 8
MLA_NOPE = 128
MLA_ROPE = 64
MLA_QK = MLA_NOPE + MLA_ROPE
MLA_V = 128
MLA_Q_RANK = 768
MLA_KV_RANK = 512
MLA_HEAD_PAD = 256
ROPE_THETA = 10000.0
MOBA_HEADS = 8
MOBA_HEAD_DIM = 128
MOBA_WIDTH = MOBA_HEADS * MOBA_HEAD_DIM
MOBA_BLOCK = 256
MOBA_TOPK = 3
PEER_HEADS = 8
PEER_NKEYS = 128
PEER_EXPERTS = PEER_NKEYS * PEER_NKEYS
PEER_DKEY = 256
PEER_TOPK = 16
EPS = 1e-6
NEG = -1e30
LOG2E = 1.4426950408889634
MOBA_GROUP = 8
MOBA_SUB = 2
KEY_CHUNK = 256
LANES = 128
VMEM_LIMIT = 56 * 1024 * 1024


def _cparams(n_axes, vmem=VMEM_LIMIT):
    return pltpu.CompilerParams(dimension_semantics=("arbitrary",) * n_axes,
                                vmem_limit_bytes=vmem)


def _rms(x):
    return x * lax.rsqrt(jnp.mean(x * x, axis=-1, keepdims=True) + EPS)


def _ada_kernel(c_ref, w_ref, b_ref, o_ref):
    c = c_ref[...]
    cs = c / (1.0 + jnp.exp(-c))
    o_ref[...] = jnp.dot(cs, w_ref[...], preferred_element_type=F32) + b_ref[...]


def _ada(c, w, b):
    bsz, d = c.shape
    n = w.shape[1]
    rows = 8
    cp = jnp.zeros((rows, d), F32).at[:bsz].set(c)
    tn = 1024
    out = pl.pallas_call(
        _ada_kernel,
        out_shape=jax.ShapeDtypeStruct((rows, n), F32),
        grid=(n // tn,),
        in_specs=[pl.BlockSpec((rows, d), lambda j: (0, 0)),
                  pl.BlockSpec((d, tn), lambda j: (0, j)),
                  pl.BlockSpec((1, tn), lambda j: (0, j))],
        out_specs=pl.BlockSpec((rows, tn), lambda j: (0, j)),
        compiler_params=_cparams(1),
        name="ada_mod",
    )(cp, w, b.reshape(1, n))
    return out[:bsz].reshape(bsz, 6, d)


def _prenorm_kernel(x_ref, g_ref, mod_ref, h_ref):
    y = _rms(x_ref[...]) * g_ref[...]
    h_ref[...] = (y * (1.0 + mod_ref[0, 1:2, :]) + mod_ref[0, 0:1, :]).astype(h_ref.dtype)


def _prenorm(x2, g, mod, seq, tm):
    t, d = x2.shape
    per_b = seq // tm
    return pl.pallas_call(
        _prenorm_kernel,
        out_shape=jax.ShapeDtypeStruct((t, d), BF16),
        grid=(t // tm,),
        in_specs=[pl.BlockSpec((tm, d), lambda i: (i, 0)),
                  pl.BlockSpec((1, d), lambda i: (0, 0)),
                  pl.BlockSpec((1, 6, d), lambda i: (i // per_b, 0, 0))],
        out_specs=pl.BlockSpec((tm, d), lambda i: (i, 0)),
        compiler_params=_cparams(1),
        name="prenorm_mod",
    )(x2, g.reshape(1, d), mod)


def _mm_kernel(a_ref, w_ref, o_ref):
    o_ref[...] = jnp.dot(a_ref[...], w_ref[...], preferred_element_type=F32).astype(o_ref.dtype)


def _matmul(a, w, tm, tn, name):
    t, k = a.shape
    n = w.shape[1]
    return pl.pallas_call(
        _mm_kernel,
        out_shape=jax.ShapeDtypeStruct((t, n), BF16),
        grid=(t // tm, n // tn),
        in_specs=[pl.BlockSpec((tm, k), lambda i, j: (i, 0)),
                  pl.BlockSpec((k, tn), lambda i, j: (0, j))],
        out_specs=pl.BlockSpec((tm, tn), lambda i, j: (i, j)),
        compiler_params=_cparams(2),
        name=name,
    )(a, w)


def _rope_half(r, cc, ss):
    return r * cc + pltpu.roll(r, 64, 1) * ss


def _mla_q_kernel(h_ref, wl_ref, g_ref, wu_ref, cc_ref, ss_ref, q_ref):
    ql = jnp.dot(h_ref[...], wl_ref[...], preferred_element_type=F32)
    qn = (_rms(ql) * g_ref[...]).astype(BF16)
    q2 = jnp.dot(qn, wu_ref[...], preferred_element_type=F32)
    cc = cc_ref[...]
    ss = ss_ref[...]
    for h in range(MLA_HEADS):
        base = h * MLA_HEAD_PAD
        q_ref[:, base:base + MLA_NOPE] = q2[:, base:base + MLA_NOPE].astype(BF16)
        r = q2[:, base + MLA_NOPE:base + MLA_HEAD_PAD]
        q_ref[:, base + MLA_NOPE:base + MLA_HEAD_PAD] = _rope_half(r, cc, ss).astype(BF16)


def _mla_q(h, wl, g, wu, cc, ss, seq, tm):
    t, d = h.shape
    per_b = seq // tm
    n = MLA_HEADS * MLA_HEAD_PAD
    return pl.pallas_call(
        _mla_q_kernel,
        out_shape=jax.ShapeDtypeStruct((t, n), BF16),
        grid=(t // tm,),
        in_specs=[pl.BlockSpec((tm, d), lambda i: (i, 0)),
                  pl.BlockSpec((d, MLA_Q_RANK), lambda i: (0, 0)),
                  pl.BlockSpec((1, MLA_Q_RANK), lambda i: (0, 0)),
                  pl.BlockSpec((MLA_Q_RANK, n), lambda i: (0, 0)),
                  pl.BlockSpec((tm, LANES), lambda i: (i % per_b, 0)),
                  pl.BlockSpec((tm, LANES), lambda i: (i % per_b, 0))],
        out_specs=pl.BlockSpec((tm, n), lambda i: (i, 0)),
        compiler_params=_cparams(1),
        name="mla_q_path",
    )(h, wl, g.reshape(1, MLA_Q_RANK), wu, cc, ss)


def _mla_kv_kernel(h_ref, wl_ref, g_ref, wu_ref, cc_ref, ss_ref, k_ref, vt_ref):
    kl = jnp.dot(h_ref[...], wl_ref[...], preferred_element_type=F32)
    kvn = (_rms(kl[:, :MLA_KV_RANK]) * g_ref[...]).astype(BF16)
    kr = _rope_half(kl[:, MLA_KV_RANK:], cc_ref[...], ss_ref[...]).astype(BF16)
    kv2 = jnp.dot(kvn, wu_ref[...], preferred_element_type=F32)
    for h in range(MLA_HEADS):
        base = h * MLA_HEAD_PAD
        k_ref[:, base:base + MLA_NOPE] = kv2[:, h * MLA_NOPE:(h + 1) * MLA_NOPE].astype(BF16)
        k_ref[:, base + MLA_NOPE:base + MLA_HEAD_PAD] = kr
    for c in range(vt_ref.shape[2]):
        v = kv2[c * KEY_CHUNK:(c + 1) * KEY_CHUNK, MLA_HEADS * MLA_NOPE:]
        vt_ref[0, :, c] = v.T.reshape(MLA_HEADS, MLA_V, KEY_CHUNK).astype(BF16)


def _mla_kv(h, wl, g, wu, cc, ss, bsz, seq, tm):
    t, d = h.shape
    per_b = seq // tm
    cpt = tm // KEY_CHUNK
    nl = MLA_KV_RANK + LANES
    nk = MLA_HEADS * MLA_HEAD_PAD
    vt_shape = (bsz, MLA_HEADS, seq // KEY_CHUNK, MLA_V, KEY_CHUNK)
    return pl.pallas_call(
        _mla_kv_kernel,
        out_shape=(jax.ShapeDtypeStruct((t, nk), BF16), jax.ShapeDtypeStruct(vt_shape, BF16)),
        grid=(t // tm,),
        in_specs=[pl.BlockSpec((tm, d), lambda i: (i, 0)),
                  pl.BlockSpec((d, nl), lambda i: (0, 0)),
                  pl.BlockSpec((1, MLA_KV_RANK), lambda i: (0, 0)),
                  pl.BlockSpec((MLA_KV_RANK, nk), lambda i: (0, 0)),
                  pl.BlockSpec((tm, LANES), lambda i: (i % per_b, 0)),
                  pl.BlockSpec((tm, LANES), lambda i: (i % per_b, 0))],
        out_specs=(pl.BlockSpec((tm, nk), lambda i: (i, 0)),
                   pl.BlockSpec((1, MLA_HEADS, cpt, MLA_V, KEY_CHUNK),
                                lambda i: (i // per_b, 0, i % per_b, 0, 0))),
        compiler_params=_cparams(1),
        name="mla_kv_path",
    )(h, wl, g.reshape(1, MLA_KV_RANK), wu, cc, ss)


def _online_update_t(carry, parts, vts, offset):
    m, l, acc = carry
    mb = functools.reduce(jnp.maximum, [jnp.max(x, axis=0, keepdims=True) for x in parts])
    m_new = jnp.maximum(m, mb + offset)
    alpha = jnp.exp2(m - m_new)
    shift = offset - m_new
    l = alpha * l
    acc = alpha * acc
    for x, vt in zip(parts, vts):
        p = jnp.exp2(x + shift)
        l = l + jnp.sum(p, axis=0, keepdims=True)
        acc = acc + jnp.dot(vt, p.astype(BF16), preferred_element_type=F32)
    return m_new, l, acc


def _attn_init(tq, dv):
    return (jnp.full((1, tq), NEG, F32), jnp.zeros((1, tq), F32), jnp.zeros((dv, tq), F32))


def _mla_attn_kernel(q_ref, k_ref, vt_ref, o_ref, *, tq):
    C = KEY_CHUNK
    per = tq // C
    i = pl.program_id(2)
    qt = q_ref[...].astype(F32).T.astype(BF16)
    zero = jnp.zeros((1, tq), F32)

    def scores(c):
        k = k_ref[pl.ds(pl.multiple_of(c * C, C), C), :]
        return jnp.dot(k, qt, preferred_element_type=F32)

    def trip(t, carry):
        tiles = [scores(t * per + u) for u in range(per)]
        for u in range(per):
            carry = _online_update_t(carry, [tiles[u]], [vt_ref[0, 0, t * per + u]], zero)
        return carry

    carry = lax.fori_loop(0, i, trip, _attn_init(tq, MLA_V))
    key = lax.broadcasted_iota(jnp.int32, (C, tq), 0)
    qry = lax.broadcasted_iota(jnp.int32, (C, tq), 1)
    for u in range(per):
        c = i * per + u
        s = jnp.where(key + u * C <= qry, scores(c), NEG)
        carry = _online_update_t(carry, [s], [vt_ref[0, 0, c]], zero)
    _, l, acc = carry
    o_ref[...] = (acc / l).T.astype(o_ref.dtype)


def _mla_attn(q, k, vt, bsz, seq, tq):
    nq = seq // tq
    nc = seq // KEY_CHUNK
    return pl.pallas_call(
        functools.partial(_mla_attn_kernel, tq=tq),
        out_shape=jax.ShapeDtypeStruct((bsz * seq, MLA_HEADS * MLA_V), BF16),
        grid=(bsz, MLA_HEADS, nq),
        in_specs=[pl.BlockSpec((tq, MLA_HEAD_PAD), lambda b, h, i: (b * nq + i, h)),
                  pl.BlockSpec((seq, MLA_HEAD_PAD), lambda b, h, i: (b, h)),
                  pl.BlockSpec((1, 1, nc, MLA_V, KEY_CHUNK), lambda b, h, i: (b, h, 0, 0, 0))],
        out_specs=pl.BlockSpec((tq, MLA_V), lambda b, h, i: (b * nq + i, h)),
        compiler_params=_cparams(3),
        name="mla_attention",
    )(q, k, vt)


def _moba_kernel(q_ref, k_ref, vt_ref, slope_ref, o_ref, kmean_sc, sel_sc, bias_sc, *,
                 n_blocks, group, sub):
    L = MOBA_BLOCK
    nbp = kmean_sc.shape[0]
    i = pl.program_id(2)
    slope = slope_ref[0]

    @pl.when(i == 0)
    def _():
        kmean_sc[...] = jnp.zeros_like(kmean_sc)
        for n in range(n_blocks):
            kb = k_ref[n * L:(n + 1) * L, :].astype(F32)
            kmean_sc[n:n + 1, :] = jnp.sum(kb, axis=0, keepdims=True) * (1.0 / L)
        bias_sc[...] = slope * lax.broadcasted_iota(jnp.int32, bias_sc.shape, 0).astype(F32)

    qt = q_ref[...].astype(F32).T.astype(BF16)
    blk = lax.broadcasted_iota(jnp.int32, (nbp, L), 0)
    gate = jnp.dot(kmean_sc[...].astype(BF16), qt, preferred_element_type=F32)
    gate = jnp.where(blk < i, gate, NEG)
    sel = jnp.zeros((nbp, L), F32)
    for _ in range(MOBA_TOPK):
        m = jnp.max(gate, axis=0, keepdims=True)
        first = jnp.min(jnp.where(gate == m, blk, nbp), axis=0, keepdims=True)
        hit = blk == first
        sel = jnp.where(hit, jnp.where(m > 0.5 * NEG, 1.0, 0.0), sel)
        gate = jnp.where(hit, -jnp.inf, gate)
    sel_sc[...] = sel

    def scores(b0, n):
        k = k_ref[pl.ds(pl.multiple_of(b0 * L, L), n * L), :]
        return jnp.dot(k, qt, preferred_element_type=F32) + bias_sc[0:n * L, :]

    key = lax.broadcasted_iota(jnp.int32, (L, L), 0)
    qry = lax.broadcasted_iota(jnp.int32, (L, L), 1)
    s = jnp.where(key <= qry, scores(i, 1), NEG)
    carry = _online_update_t(_attn_init(L, MOBA_HEAD_DIM), [s], [vt_ref[0, 0, i]],
                             jnp.zeros((1, L), F32))

    def trip(g, carry):
        tiles = [scores(g * group + u * sub, sub) for u in range(group // sub)]
        for u in range(group // sub):
            b0 = g * group + u * sub
            s = tiles[u]
            parts = [jnp.where(sel_sc[pl.ds(b0 + jj, 1), :] > 0.5, s[jj * L:(jj + 1) * L, :], NEG)
                     for jj in range(sub)]
            vts = [vt_ref[0, 0, b0 + jj] for jj in range(sub)]
            carry = _online_update_t(carry, parts, vts, slope * ((b0 - i) * L).astype(F32))
        return carry

    _, l, acc = lax.fori_loop(0, (i + group - 1) // group, trip, carry)
    o_ref[...] = (acc / l).T.astype(o_ref.dtype)


def _moba_attn(qkv, vt, slopes, bsz, seq, group, sub):
    L = MOBA_BLOCK
    nq = seq // L
    hh = MOBA_HEADS
    nbp = -(-nq // 16) * 16
    assert nq % group == 0 and group % sub == 0
    return pl.pallas_call(
        functools.partial(_moba_kernel, n_blocks=nq, group=group, sub=sub),
        out_shape=jax.ShapeDtypeStruct((bsz * seq, MOBA_WIDTH), BF16),
        grid=(bsz, hh, nq),
        in_specs=[pl.BlockSpec((L, MOBA_HEAD_DIM), lambda b, h, i: (b * nq + i, h)),
                  pl.BlockSpec((seq, MOBA_HEAD_DIM), lambda b, h, i: (b, hh + h)),
                  pl.BlockSpec((1, 1, nq, MOBA_HEAD_DIM, L), lambda b, h, i: (b, h, 0, 0, 0)),
                  pl.BlockSpec((1, 1, L), lambda b, h, i: (h, 0, 0))],
        out_specs=pl.BlockSpec((L, MOBA_HEAD_DIM), lambda b, h, i: (b * nq + i, h)),
        scratch_shapes=[pltpu.VMEM((nbp, MOBA_HEAD_DIM), F32),
                        pltpu.VMEM((nbp, L), F32),
                        pltpu.VMEM((sub * L, L), F32)],
        compiler_params=_cparams(3),
        name="moba_attention",
    )(qkv, qkv, vt, slopes)


def _outproj_kernel(oa_ref, ob_ref, ga_ref, gb_ref, w_ref, gpost_ref, x_ref, mod_ref, gffn_ref,
                    x1_ref, h2_ref, h2t_ref):
    half = oa_ref.shape[1]
    a = (_rms(oa_ref[...].astype(F32)) * ga_ref[...]).astype(BF16)
    b = (_rms(ob_ref[...].astype(F32)) * gb_ref[...]).astype(BF16)
    y = jnp.dot(a, w_ref[:half, :], preferred_element_type=F32)
    y = y + jnp.dot(b, w_ref[half:, :], preferred_element_type=F32)
    x1 = x_ref[...] + mod_ref[0, 2:3, :] * (_rms(y) * gpost_ref[...])
    x1_ref[...] = x1
    h2 = _rms(x1) * gffn_ref[...] * (1.0 + mod_ref[0, 4:5, :]) + mod_ref[0, 3:4, :]
    h2_ref[...] = h2.astype(BF16)
    h2t_ref[...] = h2.T.astype(BF16)


def _outproj(oa, ob, ga, gb, w, gpost, x2, mod, gffn, seq, tm):
    t, d = x2.shape
    half = oa.shape[1]
    per_b = seq // tm
    row = lambda n: pl.BlockSpec((1, n), lambda i: (0, 0))
    return pl.pallas_call(
        _outproj_kernel,
        out_shape=(jax.ShapeDtypeStruct((t, d), F32), jax.ShapeDtypeStruct((t, d), BF16),
                   jax.ShapeDtypeStruct((d, t), BF16)),
        grid=(t // tm,),
        in_specs=[pl.BlockSpec((tm, half), lambda i: (i, 0)),
                  pl.BlockSpec((tm, half), lambda i: (i, 0)),
                  row(half), row(half),
                  pl.BlockSpec((2 * half, d), lambda i: (0, 0)),
                  row(d),
                  pl.BlockSpec((tm, d), lambda i: (i, 0)),
                  pl.BlockSpec((1, 6, d), lambda i: (i // per_b, 0, 0)),
                  row(d)],
        out_specs=(pl.BlockSpec((tm, d), lambda i: (i, 0)),
                   pl.BlockSpec((tm, d), lambda i: (i, 0)),
                   pl.BlockSpec((d, tm), lambda i: (0, i))),
        compiler_params=_cparams(1),
        name="out_proj",
    )(oa, ob, ga.reshape(1, half), gb.reshape(1, half), w, gpost.reshape(1, d), x2, mod,
      gffn.reshape(1, d))


def _qk(q, k):
    return lax.dot_general(q, k, (((1,), (1,)), ((), ())), preferred_element_type=F32)


def _peer_select_kernel(q_ref, k1_ref, k2_ref, cnt_ref, c1_ref, rank_ref, e2_ref, top_sc, s1_sc,
                        s2_sc):
    tm = q_ref.shape[0]
    hh, kk, nk = PEER_HEADS, PEER_TOPK, PEER_NKEYS
    rowi = lax.broadcasted_iota(jnp.int32, (nk, tm), 0)

    def scores(h, p):
        c0 = (2 * h + p) * (PEER_DKEY // 2)
        kref = k1_ref if p == 0 else k2_ref
        return _qk(kref[h], q_ref[:, c0:c0 + PEER_DKEY // 2])

    clean = jnp.ones((1, tm), F32)
    for h in range(hh):
        for p in range(2):
            s = scores(h, p)
            if p == 0:
                s1_sc[h] = s
            else:
                s2_sc[h] = s
            for a in range(kk):
                m = jnp.max(s, axis=0, keepdims=True)
                top_sc[p, a, h:h + 1, :] = m
                s = jnp.where(s == m, -jnp.inf, s)
            gone = jnp.sum(jnp.where(s == -jnp.inf, 1.0, 0.0), axis=0, keepdims=True)
            clean = jnp.where(gone == float(kk), clean, 0.0)

    @pl.when(jnp.min(clean) < 0.5)
    def _():
        for h in range(hh):
            for p in range(2):
                s = s1_sc[h] if p == 0 else s2_sc[h]
                for a in range(kk):
                    m = jnp.max(s, axis=0, keepdims=True)
                    top_sc[p, a, h:h + 1, :] = m
                    first = jnp.min(jnp.where(s == m, rowi, nk), axis=0, keepdims=True)
                    s = jnp.where(rowi == first, -jnp.inf, s)

    v1 = [top_sc[0, a] for a in range(kk)]
    v2 = [top_sc[1, a] for a in range(kk)]
    cands = [v1[a] + v2[b] for a in range(kk) for b in range(kk) if (a + 1) * (b + 1) <= kk]
    best = v1[0] + v2[0]
    z = jnp.zeros_like(best)
    cur = best
    for r in range(kk):
        cur = functools.reduce(jnp.maximum, cands)
        z = z + jnp.exp(cur - best)
        if r + 1 < kk:
            todo = jnp.ones_like(best)
            nxt = []
            for c in cands:
                hit = jnp.where(c == cur, todo, 0.0)
                nxt.append(jnp.where(hit > 0.5, -jnp.inf, c))
                todo = todo - hit
            cands = nxt
    inv_z = 1.0 / z
    for h in range(hh):
        s1 = s1_sc[h]
        s2 = s2_sc[h]
        tau = cur[h:h + 1, :]
        cnt = jnp.zeros((nk, tm), F32)
        rank = jnp.zeros((nk, tm), F32)
        for b in range(kk):
            v = v2[b][h:h + 1, :]
            cnt = cnt + jnp.where(s1 + v >= tau, 1.0, 0.0)
            rank = rank + jnp.where(v > s2, 1.0, 0.0)
        cnt_ref[h] = cnt
        rank_ref[h] = rank.astype(BF16)
        c1_ref[h] = jnp.exp(s1 - v1[0][h:h + 1, :]) * inv_z[h:h + 1, :]
        e2_ref[h] = jnp.exp(s2 - v2[0][h:h + 1, :]).astype(BF16)


def _peer_select(q, k1, k2, tm):
    t, d = q.shape
    hh, nk, kk = PEER_HEADS, PEER_NKEYS, PEER_TOPK
    keys = pl.BlockSpec((hh, nk, PEER_DKEY // 2), lambda i: (0, 0, 0))
    sc = pl.BlockSpec((hh, nk, tm), lambda i: (0, 0, i))
    table = jax.ShapeDtypeStruct((hh, nk, t), F32)
    table16 = jax.ShapeDtypeStruct((hh, nk, t), BF16)
    return pl.pallas_call(
        _peer_select_kernel,
        out_shape=(table, table, table16, table16),
        grid=(t // tm,),
        in_specs=[pl.BlockSpec((tm, d), lambda i: (i, 0)), keys, keys],
        out_specs=(sc, sc, sc, sc),
        scratch_shapes=[pltpu.VMEM((2, kk, hh, tm), F32),
                        pltpu.VMEM((hh, nk, tm), F32),
                        pltpu.VMEM((hh, nk, tm), F32)],
        compiler_params=_cparams(1),
        name="peer_select",
    )(q, k1, k2)


def _peer_main_kernel(h2t_ref, u_ref, vt_ref, cnt_ref, c1_ref, rank_ref, e2_ref, x1_ref, mod_ref,
                      g_ref, o_ref, acc_sc, p_sc, *, eb):
    hh, nk = PEER_HEADS, PEER_NKEYS
    e = pl.program_id(1)
    groups = eb // nk

    @pl.when(e == 0)
    def _():
        acc_sc[...] = jnp.zeros_like(acc_sc)

    a = jnp.dot(u_ref[...], h2t_ref[...], preferred_element_type=F32)
    for g in range(groups):
        i1 = e * groups + g
        w = jnp.zeros((nk, a.shape[1]), BF16)
        for h in range(hh):
            picked = rank_ref[h] < cnt_ref[h, pl.ds(i1, 1), :].astype(BF16)
            e2 = e2_ref[h]
            w = w + jnp.where(picked, e2, jnp.zeros_like(e2)) * c1_ref[h, pl.ds(i1, 1), :].astype(BF16)
        ag = a[g * nk:(g + 1) * nk, :]
        gelu = 0.5 * ag * (1.0 + lax.erf(ag * (1.0 / math.sqrt(2.0))))
        p_sc[g * nk:(g + 1) * nk, :] = (gelu * w.astype(F32)).astype(BF16)
    acc_sc[...] += jnp.dot(vt_ref[...], p_sc[...], preferred_element_type=F32)

    @pl.when(e == pl.num_programs(1) - 1)
    def _():
        y = acc_sc[...].T
        o_ref[...] = x1_ref[...] + mod_ref[0, 5:6, :] * (_rms(y) * g_ref[...])


def _peer_main(h2t, u, vt, cnt, c1, rank, e2, x1, mod, g, seq, tm, eb):
    d, t = h2t.shape
    hh, nk = PEER_HEADS, PEER_NKEYS
    per_b = seq // tm
    once = pl.Buffered(1)
    sc = pl.BlockSpec((hh, nk, tm), lambda i, e: (0, 0, i), pipeline_mode=once)
    return pl.pallas_call(
        functools.partial(_peer_main_kernel, eb=eb),
        out_shape=jax.ShapeDtypeStruct((t, d), F32),
        grid=(t // tm, PEER_EXPERTS // eb),
        in_specs=[pl.BlockSpec((d, tm), lambda i, e: (0, i)),
                  pl.BlockSpec((eb, d), lambda i, e: (e, 0)),
                  pl.BlockSpec((d, eb), lambda i, e: (0, e)),
                  sc, sc, sc, sc,
                  pl.BlockSpec((tm, d), lambda i, e: (i, 0), pipeline_mode=once),
                  pl.BlockSpec((1, 6, d), lambda i, e: (i // per_b, 0, 0)),
                  pl.BlockSpec((1, d), lambda i, e: (0, 0))],
        out_specs=pl.BlockSpec((tm, d), lambda i, e: (i, 0)),
        scratch_shapes=[pltpu.VMEM((d, tm), F32),
                        pltpu.VMEM((eb, tm), BF16)],
        compiler_params=_cparams(2),
        name="peer_experts",
    )(h2t, u, vt, cnt, c1, rank, e2, x1, mod, g.reshape(1, d))


def _rope_tables(seq):
    half = MLA_ROPE // 2
    inv_freq = ROPE_THETA ** (-jnp.arange(half, dtype=F32) / half)
    ang = jnp.arange(seq, dtype=F32)[:, None] * inv_freq[None, :]
    zeros = jnp.zeros((seq, LANES - MLA_ROPE), F32)
    cc = jnp.concatenate([jnp.cos(ang), jnp.cos(ang), zeros], axis=-1)
    ss = jnp.concatenate([jnp.sin(ang), jnp.sin(ang), zeros], axis=-1)
    return cc, ss


def _rot_cols(w):
    half = MLA_ROPE // 2
    return jnp.concatenate([-w[..., half:], w[..., :half]], axis=-1)


def _layer(x2, mod, bsz, seq, w_in, g_pre_mix, g_q_lat, w_uq, g_kv_lat, w_ukv, g_out_mla,
           g_out_moba, w_out, g_post_mix, g_pre_ffn, w_peer_q, sub_keys_1, sub_keys_2,
           u_experts, v_experts, g_post_ffn):
    t = bsz * seq
    tm = min(512, seq)

    o_q, o_kv, o_kr, o_mq, o_mk, o_mv = 0, 768, 1280, 1344, 2368, 3392
    w_ql = w_in[:, o_q:o_kv].astype(BF16)
    w_kr = w_in[:, o_kr:o_mq]
    w_kvl = jnp.concatenate([w_in[:, o_kv:o_kr], w_kr, _rot_cols(w_kr)], axis=-1).astype(BF16)
    moba_scale = LOG2E / math.sqrt(MOBA_HEAD_DIM)
    w_m = jnp.concatenate([w_in[:, o_mq:o_mk] * moba_scale, w_in[:, o_mk:]], axis=-1).astype(BF16)
    wq3 = w_uq.reshape(MLA_Q_RANK, MLA_HEADS, MLA_QK) * (LOG2E / math.sqrt(MLA_QK))
    wq_rope = wq3[..., MLA_NOPE:]
    w_uq_p = jnp.concatenate([wq3[..., :MLA_NOPE], wq_rope, _rot_cols(wq_rope)], axis=-1)
    w_uq_p = w_uq_p.reshape(MLA_Q_RANK, MLA_HEADS * MLA_HEAD_PAD).astype(BF16)
    wkv3 = w_ukv.reshape(MLA_KV_RANK, MLA_HEADS, MLA_NOPE + MLA_V)
    w_ukv_p = jnp.concatenate([wkv3[..., :MLA_NOPE].reshape(MLA_KV_RANK, -1),
                               wkv3[..., MLA_NOPE:].reshape(MLA_KV_RANK, -1)], axis=-1).astype(BF16)
    cc, ss = _rope_tables(seq)
    slopes = LOG2E * 2.0 ** (-8.0 * (jnp.arange(MOBA_HEADS, dtype=F32) + 1.0) / MOBA_HEADS)
    slopes = jnp.broadcast_to(slopes[:, None, None], (MOBA_HEADS, 1, MOBA_BLOCK))

    h = _prenorm(x2, g_pre_mix, mod, seq, tm)
    q = _mla_q(h, w_ql, g_q_lat, w_uq_p, cc, ss, seq, tm)
    k, vt = _mla_kv(h, w_kvl, g_kv_lat, w_ukv_p, cc, ss, bsz, seq, tm)
    qkv_m = _matmul(h, w_m, min(1024, t), 1024, "moba_qkv_proj")
    o_mla = _mla_attn(q, k, vt, bsz, seq, min(1024, seq))
    n_blk = seq // MOBA_BLOCK
    vt_m = qkv_m[:, 2 * MOBA_WIDTH:].reshape(bsz, n_blk, MOBA_BLOCK, MOBA_HEADS, MOBA_HEAD_DIM)
    vt_m = vt_m.transpose(0, 3, 1, 4, 2)
    o_moba = _moba_attn(qkv_m, vt_m, slopes, bsz, seq, MOBA_GROUP, MOBA_SUB)
    x1, h2, h2t = _outproj(o_mla, o_moba, g_out_mla, g_out_moba, w_out.astype(BF16), g_post_mix,
                           x2, mod, g_pre_ffn, seq, tm)

    pq = _matmul(h2, w_peer_q.astype(BF16), min(1024, t), 1024, "peer_q_proj")
    cnt, c1, rank, e2 = _peer_select(pq, sub_keys_1.astype(BF16), sub_keys_2.astype(BF16),
                                     min(256, t))
    return _peer_main(h2t, u_experts.astype(BF16), v_experts.T.astype(BF16), cnt, c1, rank, e2, x1,
                      mod, g_post_ffn, seq, tm, 1024)


def kernel(x, c, w_ada, b_ada, g_pre_mix, w_in, g_q_lat, w_uq, g_kv_lat, w_ukv, g_out_mla,
           g_out_moba, w_out, g_post_mix, g_pre_ffn, w_peer_q, sub_keys_1, sub_keys_2, u_experts,
           v_experts, g_post_ffn):
    bsz, seq, d = x.shape
    x2 = x.reshape(bsz * seq, d)
    for l in range(w_ada.shape[0]):
        mod = _ada(c, w_ada[l], b_ada[l])
        x2 = _layer(x2, mod, bsz, seq, w_in[l], g_pre_mix[l], g_q_lat[l], w_uq[l], g_kv_lat[l],
                    w_ukv[l], g_out_mla[l], g_out_moba[l], w_out[l], g_post_mix[l], g_pre_ffn[l],
                    w_peer_q[l], sub_keys_1[l], sub_keys_2[l], u_experts[l], v_experts[l],
                    g_post_ffn[l])
    return x2.reshape(bsz, seq, d)
```

```python
import functools
import math

import jax
import jax.numpy as jnp
from jax import lax
from jax.experimental import pallas as pl
from jax.experimental.pallas import tpu as pltpu

F32 = jnp.float32
BF16 = jnp.bfloat16

D_MODEL = 2048
MLA_HEADS = 8
MLA_NOPE = 128
MLA_ROPE = 64
MLA_QK = MLA_NOPE + MLA_ROPE
MLA_V = 128
MLA_Q_RANK = 768
MLA_KV_RANK = 512
MLA_HEAD_PAD = 256
ROPE_THETA = 10000.0
MOBA_HEADS = 8
MOBA_HEAD_DIM = 128
MOBA_WIDTH = MOBA_HEADS * MOBA_HEAD_DIM
MOBA_BLOCK = 256
MOBA_TOPK = 3
PEER_HEADS = 8
PEER_NKEYS = 128
PEER_EXPERTS = PEER_NKEYS * PEER_NKEYS
PEER_DKEY = 256
PEER_TOPK = 16
EPS = 1e-6
NEG = -1e30
LOG2E = 1.4426950408889634
MOBA_GROUP = 8
MOBA_SUB = 2
KEY_CHUNK = 512
LANES = 128
VMEM_LIMIT = 56 * 1024 * 1024


def _cparams(n_axes, vmem=VMEM_LIMIT):
    return pltpu.CompilerParams(dimension_semantics=("arbitrary",) * n_axes,
                                vmem_limit_bytes=vmem)


def _rms(x):
    return x * lax.rsqrt(jnp.mean(x * x, axis=-1, keepdims=True) + EPS)


def _ada_kernel(c_ref, w_ref, b_ref, o_ref):
    c = c_ref[...]
    cs = c / (1.0 + jnp.exp(-c))
    o_ref[...] = jnp.dot(cs, w_ref[...], preferred_element_type=F32) + b_ref[...]


def _ada(c, w, b):
    bsz, d = c.shape
    n = w.shape[1]
    rows = 8
    cp = jnp.zeros((rows, d), F32).at[:bsz].set(c)
    tn = 1024
    out = pl.pallas_call(
        _ada_kernel,
        out_shape=jax.ShapeDtypeStruct((rows, n), F32),
        grid=(n // tn,),
        in_specs=[pl.BlockSpec((rows, d), lambda j: (0, 0)),
                  pl.BlockSpec((d, tn), lambda j: (0, j)),
                  pl.BlockSpec((1, tn), lambda j: (0, j))],
        out_specs=pl.BlockSpec((rows, tn), lambda j: (0, j)),
        compiler_params=_cparams(1),
        name="ada_mod",
    )(cp, w, b.reshape(1, n))
    return out[:bsz].reshape(bsz, 6, d)


def _prenorm_kernel(x_ref, g_ref, mod_ref, h_ref):
    y = _rms(x_ref[...]) * g_ref[...]
    h_ref[...] = (y * (1.0 + mod_ref[0, 1:2, :]) + mod_ref[0, 0:1, :]).astype(h_ref.dtype)


def _prenorm(x2, g, mod, seq, tm):
    t, d = x2.shape
    per_b = seq // tm
    return pl.pallas_call(
        _prenorm_kernel,
        out_shape=jax.ShapeDtypeStruct((t, d), BF16),
        grid=(t // tm,),
        in_specs=[pl.BlockSpec((tm, d), lambda i: (i, 0)),
                  pl.BlockSpec((1, d), lambda i: (0, 0)),
                  pl.BlockSpec((1, 6, d), lambda i: (i // per_b, 0, 0))],
        out_specs=pl.BlockSpec((tm, d), lambda i: (i, 0)),
        compiler_params=_cparams(1),
        name="prenorm_mod",
    )(x2, g.reshape(1, d), mod)


def _mm_kernel(a_ref, w_ref, o_ref):
    o_ref[...] = jnp.dot(a_ref[...], w_ref[...], preferred_element_type=F32).astype(o_ref.dtype)


def _matmul(a, w, tm, tn, name):
    t, k = a.shape
    n = w.shape[1]
    return pl.pallas_call(
        _mm_kernel,
        out_shape=jax.ShapeDtypeStruct((t, n), BF16),
        grid=(t // tm, n // tn),
        in_specs=[pl.BlockSpec((tm, k), lambda i, j: (i, 0)),
                  pl.BlockSpec((k, tn), lambda i, j: (0, j))],
        out_specs=pl.BlockSpec((tm, tn), lambda i, j: (i, j)),
        compiler_params=_cparams(2),
        name=name,
    )(a, w)


def _rope_half(r, cc, ss):
    return r * cc + pltpu.roll(r, 64, 1) * ss


def _mla_q_kernel(h_ref, wl_ref, g_ref, wu_ref, cc_ref, ss_ref, q_ref):
    ql = jnp.dot(h_ref[...], wl_ref[...], preferred_element_type=F32)
    qn = (_rms(ql) * g_ref[...]).astype(BF16)
    q2 = jnp.dot(qn, wu_ref[...], preferred_element_type=F32)
    cc = cc_ref[...]
    ss = ss_ref[...]
    for h in range(MLA_HEADS):
        base = h * MLA_HEAD_PAD
        q_ref[:, base:base + MLA_NOPE] = q2[:, base:base + MLA_NOPE].astype(BF16)
        r = q2[:, base + MLA_NOPE:base + MLA_HEAD_PAD]
        q_ref[:, base + MLA_NOPE:base + MLA_HEAD_PAD] = _rope_half(r, cc, ss).astype(BF16)


def _mla_q(h, wl, g, wu, cc, ss, seq, tm):
    t, d = h.shape
    per_b = seq // tm
    n = MLA_HEADS * MLA_HEAD_PAD
    return pl.pallas_call(
        _mla_q_kernel,
        out_shape=jax.ShapeDtypeStruct((t, n), BF16),
        grid=(t // tm,),
        in_specs=[pl.BlockSpec((tm, d), lambda i: (i, 0)),
                  pl.BlockSpec((d, MLA_Q_RANK), lambda i: (0, 0)),
                  pl.BlockSpec((1, MLA_Q_RANK), lambda i: (0, 0)),
                  pl.BlockSpec((MLA_Q_RANK, n), lambda i: (0, 0)),
                  pl.BlockSpec((tm, LANES), lambda i: (i % per_b, 0)),
                  pl.BlockSpec((tm, LANES), lambda i: (i % per_b, 0))],
        out_specs=pl.BlockSpec((tm, n), lambda i: (i, 0)),
        compiler_params=_cparams(1),
        name="mla_q_path",
    )(h, wl, g.reshape(1, MLA_Q_RANK), wu, cc, ss)


def _mla_kv_kernel(h_ref, wl_ref, g_ref, wu_ref, cc_ref, ss_ref, k_ref, vt_ref):
    kl = jnp.dot(h_ref[...], wl_ref[...], preferred_element_type=F32)
    kvn = (_rms(kl[:, :MLA_KV_RANK]) * g_ref[...]).astype(BF16)
    kr = _rope_half(kl[:, MLA_KV_RANK:], cc_ref[...], ss_ref[...]).astype(BF16)
    kv2 = jnp.dot(kvn, wu_ref[...], preferred_element_type=F32)
    for h in range(MLA_HEADS):
        base = h * MLA_HEAD_PAD
        k_ref[:, base:base + MLA_NOPE] = kv2[:, h * MLA_NOPE:(h + 1) * MLA_NOPE].astype(BF16)
        k_ref[:, base + MLA_NOPE:base + MLA_HEAD_PAD] = kr
    for c in range(vt_ref.shape[2]):
        v = kv2[c * KEY_CHUNK:(c + 1) * KEY_CHUNK, MLA_HEADS * MLA_NOPE:]
        vt_ref[0, :, c] = v.T.reshape(MLA_HEADS, MLA_V, KEY_CHUNK).astype(BF16)


def _mla_kv(h, wl, g, wu, cc, ss, bsz, seq, tm):
    t, d = h.shape
    per_b = seq // tm
    cpt = tm // KEY_CHUNK
    nl = MLA_KV_RANK + LANES
    nk = MLA_HEADS * MLA_HEAD_PAD
    vt_shape = (bsz, MLA_HEADS, seq // KEY_CHUNK, MLA_V, KEY_CHUNK)
    return pl.pallas_call(
        _mla_kv_kernel,
        out_shape=(jax.ShapeDtypeStruct((t, nk), BF16), jax.ShapeDtypeStruct(vt_shape, BF16)),
        grid=(t // tm,),
        in_specs=[pl.BlockSpec((tm, d), lambda i: (i, 0)),
                  pl.BlockSpec((d, nl), lambda i: (0, 0)),
                  pl.BlockSpec((1, MLA_KV_RANK), lambda i: (0, 0)),
                  pl.BlockSpec((MLA_KV_RANK, nk), lambda i: (0, 0)),
                  pl.BlockSpec((tm, LANES), lambda i: (i % per_b, 0)),
                  pl.BlockSpec((tm, LANES), lambda i: (i % per_b, 0))],
        out_specs=(pl.BlockSpec((tm, nk), lambda i: (i, 0)),
                   pl.BlockSpec((1, MLA_HEADS, cpt, MLA_V, KEY_CHUNK),
                                lambda i: (i // per_b, 0, i % per_b, 0, 0))),
        compiler_params=_cparams(1),
        name="mla_kv_path",
    )(h, wl, g.reshape(1, MLA_KV_RANK), wu, cc, ss)


def _online_update_t(carry, parts, vts, offset):
    m, l, acc = carry
    mb = functools.reduce(jnp.maximum, [jnp.max(x, axis=0, keepdims=True) for x in parts])
    m_new = jnp.maximum(m, mb + offset)
    alpha = jnp.exp2(m - m_new)
    shift = offset - m_new
    l = alpha * l
    acc = alpha * acc
    for x, vt in zip(parts, vts):
        p = jnp.exp2(x + shift)
        l = l + jnp.sum(p, axis=0, keepdims=True)
        acc = acc + jnp.dot(vt, p.astype(BF16), preferred_element_type=F32)
    return m_new, l, acc


def _attn_init(tq, dv):
    return (jnp.full((1, tq), NEG, F32), jnp.zeros((1, tq), F32), jnp.zeros((dv, tq), F32))


def _mla_attn_kernel(q_ref, k_ref, vt_ref, o_ref, *, tq):
    C = KEY_CHUNK
    per = tq // C
    i = pl.program_id(2)
    qt = q_ref[...].astype(F32).T.astype(BF16)
    zero = jnp.zeros((1, tq), F32)

    def scores(c):
        k = k_ref[pl.ds(pl.multiple_of(c * C, C), C), :]
        return jnp.dot(k, qt, preferred_element_type=F32)

    def trip(t, carry):
        tiles = [scores(t * per + u) for u in range(per)]
        for u in range(per):
            carry = _online_update_t(carry, [tiles[u]], [vt_ref[0, 0, t * per + u]], zero)
        return carry

    carry = lax.fori_loop(0, i, trip, _attn_init(tq, MLA_V))
    key = lax.broadcasted_iota(jnp.int32, (C, tq), 0)
    qry = lax.broadcasted_iota(jnp.int32, (C, tq), 1)
    for u in range(per):
        c = i * per + u
        s = jnp.where(key + u * C <= qry, scores(c), NEG)
        carry = _online_update_t(carry, [s], [vt_ref[0, 0, c]], zero)
    _, l, acc = carry
    o_ref[...] = (acc / l).T.astype(o_ref.dtype)


def _mla_attn(q, k, vt, bsz, seq, tq):
    nq = seq // tq
    nc = seq // KEY_CHUNK
    return pl.pallas_call(
        functools.partial(_mla_attn_kernel, tq=tq),
        out_shape=jax.ShapeDtypeStruct((bsz * seq, MLA_HEADS * MLA_V), BF16),
        grid=(bsz, MLA_HEADS, nq),
        in_specs=[pl.BlockSpec((tq, MLA_HEAD_PAD), lambda b, h, i: (b * nq + i, h)),
                  pl.BlockSpec((seq, MLA_HEAD_PAD), lambda b, h, i: (b, h)),
                  pl.BlockSpec((1, 1, nc, MLA_V, KEY_CHUNK), lambda b, h, i: (b, h, 0, 0, 0))],
        out_specs=pl.BlockSpec((tq, MLA_V), lambda b, h, i: (b * nq + i, h)),
        compiler_params=_cparams(3),
        name="mla_attention",
    )(q, k, vt)


def _moba_kernel(q_ref, k_ref, vt_ref, slope_ref, o_ref, kmean_sc, sel_sc, bias_sc, *,
                 n_blocks, group, sub):
    L = MOBA_BLOCK
    nbp = kmean_sc.shape[0]
    i = pl.program_id(2)
    slope = slope_ref[0]

    @pl.when(i == 0)
    def _():
        kmean_sc[...] = jnp.zeros_like(kmean_sc)
        for n in range(n_blocks):
            kb = k_ref[n * L:(n + 1) * L, :].astype(F32)
            kmean_sc[n:n + 1, :] = jnp.sum(kb, axis=0, keepdims=True) * (1.0 / L)
        bias_sc[...] = slope * lax.broadcasted_iota(jnp.int32, bias_sc.shape, 0).astype(F32)

    qt = q_ref[...].astype(F32).T.astype(BF16)
    blk = lax.broadcasted_iota(jnp.int32, (nbp, L), 0)
    gate = jnp.dot(kmean_sc[...].astype(BF16), qt, preferred_element_type=F32)
    gate = jnp.where(blk < i, gate, NEG)
    sel = jnp.zeros((nbp, L), F32)
    for _ in range(MOBA_TOPK):
        m = jnp.max(gate, axis=0, keepdims=True)
        first = jnp.min(jnp.where(gate == m, blk, nbp), axis=0, keepdims=True)
        hit = blk == first
        sel = jnp.where(hit, jnp.where(m > 0.5 * NEG, 1.0, 0.0), sel)
        gate = jnp.where(hit, -jnp.inf, gate)
    sel_sc[...] = sel

    def scores(b0, n):
        k = k_ref[pl.ds(pl.multiple_of(b0 * L, L), n * L), :]
        return jnp.dot(k, qt, preferred_element_type=F32) + bias_sc[0:n * L, :]

    key = lax.broadcasted_iota(jnp.int32, (L, L), 0)
    qry = lax.broadcasted_iota(jnp.int32, (L, L), 1)
    s = jnp.where(key <= qry, scores(i, 1), NEG)
    carry = _online_update_t(_attn_init(L, MOBA_HEAD_DIM), [s], [vt_ref[0, 0, i]],
                             jnp.zeros((1, L), F32))

    def trip(g, carry):
        tiles = [scores(g * group + u * sub, sub) for u in range(group // sub)]
        for u in range(group // sub):
            b0 = g * group + u * sub
            s = tiles[u]
            parts = [jnp.where(sel_sc[pl.ds(b0 + jj, 1), :] > 0.5, s[jj * L:(jj + 1) * L, :], NEG)
                     for jj in range(sub)]
            vts = [vt_ref[0, 0, b0 + jj] for jj in range(sub)]
            carry = _online_update_t(carry, parts, vts, slope * ((b0 - i) * L).astype(F32))
        return carry

    _, l, acc = lax.fori_loop(0, (i + group - 1) // group, trip, carry)
    o_ref[...] = (acc / l).T.astype(o_ref.dtype)


def _moba_attn(qkv, vt, slopes, bsz, seq, group, sub):
    L = MOBA_BLOCK
    nq = seq // L
    hh = MOBA_HEADS
    nbp = -(-nq // 16) * 16
    assert nq % group == 0 and group % sub == 0
    return pl.pallas_call(
        functools.partial(_moba_kernel, n_blocks=nq, group=group, sub=sub),
        out_shape=jax.ShapeDtypeStruct((bsz * seq, MOBA_WIDTH), BF16),
        grid=(bsz, hh, nq),
        in_specs=[pl.BlockSpec((L, MOBA_HEAD_DIM), lambda b, h, i: (b * nq + i, h)),
                  pl.BlockSpec((seq, MOBA_HEAD_DIM), lambda b, h, i: (b, hh + h)),
                  pl.BlockSpec((1, 1, nq, MOBA_HEAD_DIM, L), lambda b, h, i: (b, h, 0, 0, 0)),
                  pl.BlockSpec((1, 1, L), lambda b, h, i: (h, 0, 0))],
        out_specs=pl.BlockSpec((L, MOBA_HEAD_DIM), lambda b, h, i: (b * nq + i, h)),
        scratch_shapes=[pltpu.VMEM((nbp, MOBA_HEAD_DIM), F32),
                        pltpu.VMEM((nbp, L), F32),
                        pltpu.VMEM((sub * L, L), F32)],
        compiler_params=_cparams(3),
        name="moba_attention",
    )(qkv, qkv, vt, slopes)


def _outproj_kernel(oa_ref, ob_ref, ga_ref, gb_ref, w_ref, gpost_ref, x_ref, mod_ref, gffn_ref,
                    x1_ref, h2_ref, h2t_ref):
    half = oa_ref.shape[1]
    a = (_rms(oa_ref[...].astype(F32)) * ga_ref[...]).astype(BF16)
    b = (_rms(ob_ref[...].astype(F32)) * gb_ref[...]).astype(BF16)
    y = jnp.dot(a, w_ref[:half, :], preferred_element_type=F32)
    y = y + jnp.dot(b, w_ref[half:, :], preferred_element_type=F32)
    x1 = x_ref[...] + mod_ref[0, 2:3, :] * (_rms(y) * gpost_ref[...])
    x1_ref[...] = x1
    h2 = _rms(x1) * gffn_ref[...] * (1.0 + mod_ref[0, 4:5, :]) + mod_ref[0, 3:4, :]
    h2_ref[...] = h2.astype(BF16)
    h2t_ref[...] = h2.T.astype(BF16)


def _outproj(oa, ob, ga, gb, w, gpost, x2, mod, gffn, seq, tm):
    t, d = x2.shape
    half = oa.shape[1]
    per_b = seq // tm
    row = lambda n: pl.BlockSpec((1, n), lambda i: (0, 0))
    return pl.pallas_call(
        _outproj_kernel,
        out_shape=(jax.ShapeDtypeStruct((t, d), F32), jax.ShapeDtypeStruct((t, d), BF16),
                   jax.ShapeDtypeStruct((d, t), BF16)),
        grid=(t // tm,),
        in_specs=[pl.BlockSpec((tm, half), lambda i: (i, 0)),
                  pl.BlockSpec((tm, half), lambda i: (i, 0)),
                  row(half), row(half),
                  pl.BlockSpec((2 * half, d), lambda i: (0, 0)),
                  row(d),
                  pl.BlockSpec((tm, d), lambda i: (i, 0)),
                  pl.BlockSpec((1, 6, d), lambda i: (i // per_b, 0, 0)),
                  row(d)],
        out_specs=(pl.BlockSpec((tm, d), lambda i: (i, 0)),
                   pl.BlockSpec((tm, d), lambda i: (i, 0)),
                   pl.BlockSpec((d, tm), lambda i: (0, i))),
        compiler_params=_cparams(1),
        name="out_proj",
    )(oa, ob, ga.reshape(1, half), gb.reshape(1, half), w, gpost.reshape(1, d), x2, mod,
      gffn.reshape(1, d))


def _qk(q, k):
    return lax.dot_general(q, k, (((1,), (1,)), ((), ())), preferred_element_type=F32)


def _peer_select_kernel(q_ref, k1_ref, k2_ref, cnt_ref, c1_ref, rank_ref, e2_ref, top_sc, s1_sc,
                        s2_sc, rank_sc):
    tm = q_ref.shape[0]
    hh, kk, nk = PEER_HEADS, PEER_TOPK, PEER_NKEYS
    rowi = lax.broadcasted_iota(jnp.int32, (nk, tm), 0)

    def scores(h, p):
        c0 = (2 * h + p) * (PEER_DKEY // 2)
        kref = k1_ref if p == 0 else k2_ref
        return _qk(kref[h], q_ref[:, c0:c0 + PEER_DKEY // 2])

    clean = jnp.ones((1, tm), F32)
    for h in range(hh):
        for p in range(2):
            s = scores(h, p)
            if p == 0:
                s1_sc[h] = s
            else:
                s2_sc[h] = s
            rank = jnp.full((nk, tm), float(kk), F32)
            for a in range(kk):
                m = jnp.max(s, axis=0, keepdims=True)
                top_sc[p, a, h:h + 1, :] = m
                hit = s == m
                s = jnp.where(hit, -jnp.inf, s)
                if p == 1:
                    rank = jnp.where(hit, float(a), rank)
            if p == 1:
                rank_sc[h] = rank
            gone = jnp.sum(jnp.where(s == -jnp.inf, 1.0, 0.0), axis=0, keepdims=True)
            clean = jnp.where(gone == float(kk), clean, 0.0)

    @pl.when(jnp.min(clean) < 0.5)
    def _():
        for h in range(hh):
            for p in range(2):
                s = s1_sc[h] if p == 0 else s2_sc[h]
                rank = jnp.full((nk, tm), float(kk), F32)
                for a in range(kk):
                    m = jnp.max(s, axis=0, keepdims=True)
                    top_sc[p, a, h:h + 1, :] = m
                    first = jnp.min(jnp.where(s == m, rowi, nk), axis=0, keepdims=True)
                    hit = rowi == first
                    s = jnp.where(hit, -jnp.inf, s)
                    if p == 1:
                        rank = jnp.where(hit, float(a), rank)
                if p == 1:
                    rank_sc[h] = rank

    v1 = [top_sc[0, a] for a in range(kk)]
    v2 = [top_sc[1, a] for a in range(kk)]
    cands = [v1[a] + v2[b] for a in range(kk) for b in range(kk) if (a + 1) * (b + 1) <= kk]
    best = v1[0] + v2[0]
    z = jnp.zeros_like(best)
    cur = best
    for r in range(kk):
        cur = functools.reduce(jnp.maximum, cands)
        z = z + jnp.exp(cur - best)
        if r + 1 < kk:
            todo = jnp.ones_like(best)
            nxt = []
            for c in cands:
                hit = jnp.where(c == cur, todo, 0.0)
                nxt.append(jnp.where(hit > 0.5, -jnp.inf, c))
                todo = todo - hit
            cands = nxt
    inv_z = 1.0 / z
    for h in range(hh):
        s1 = s1_sc[h]
        s2 = s2_sc[h]
        tau = cur[h:h + 1, :]
        cnt = jnp.zeros((nk, tm), F32)
        for b in range(kk):
            cnt = cnt + jnp.where(s1 + v2[b][h:h + 1, :] >= tau, 1.0, 0.0)
        cnt_ref[h] = cnt
        rank_ref[h] = rank_sc[h].astype(BF16)
        c1_ref[h] = jnp.exp(s1 - v1[0][h:h + 1, :]) * inv_z[h:h + 1, :]
        e2_ref[h] = jnp.exp(s2 - v2[0][h:h + 1, :]).astype(BF16)


def _peer_select(q, k1, k2, tm):
    t, d = q.shape
    hh, nk, kk = PEER_HEADS, PEER_NKEYS, PEER_TOPK
    keys = pl.BlockSpec((hh, nk, PEER_DKEY // 2), lambda i: (0, 0, 0))
    sc = pl.BlockSpec((hh, nk, tm), lambda i: (0, 0, i))
    table = jax.ShapeDtypeStruct((hh, nk, t), F32)
    table16 = jax.ShapeDtypeStruct((hh, nk, t), BF16)
    return pl.pallas_call(
        _peer_select_kernel,
        out_shape=(table, table, table16, table16),
        grid=(t // tm,),
        in_specs=[pl.BlockSpec((tm, d), lambda i: (i, 0)), keys, keys],
        out_specs=(sc, sc, sc, sc),
        scratch_shapes=[pltpu.VMEM((2, kk, hh, tm), F32),
                        pltpu.VMEM((hh, nk, tm), F32),
                        pltpu.VMEM((hh, nk, tm), F32),
                        pltpu.VMEM((hh, nk, tm), F32)],
        compiler_params=_cparams(1),
        name="peer_select",
    )(q, k1, k2)


def _peer_main_kernel(h2t_ref, u_ref, vt_ref, cnt_ref, c1_ref, rank_ref, e2_ref, x1_ref, mod_ref,
                      g_ref, o_ref, acc_sc, p_sc, *, eb):
    hh, nk = PEER_HEADS, PEER_NKEYS
    e = pl.program_id(1)
    groups = eb // nk

    @pl.when(e == 0)
    def _():
        acc_sc[...] = jnp.zeros_like(acc_sc)

    a = jnp.dot(u_ref[...], h2t_ref[...], preferred_element_type=F32)
    for g in range(groups):
        i1 = e * groups + g
        w = jnp.zeros((nk, a.shape[1]), BF16)
        for h in range(hh):
            picked = rank_ref[h] < cnt_ref[h, pl.ds(i1, 1), :].astype(BF16)
            e2 = e2_ref[h]
            w = w + jnp.where(picked, e2, jnp.zeros_like(e2)) * c1_ref[h, pl.ds(i1, 1), :].astype(BF16)
        ag = a[g * nk:(g + 1) * nk, :]
        gelu = 0.5 * ag * (1.0 + lax.erf(ag * (1.0 / math.sqrt(2.0))))
        p_sc[g * nk:(g + 1) * nk, :] = (gelu * w.astype(F32)).astype(BF16)
    acc_sc[...] += jnp.dot(vt_ref[...], p_sc[...], preferred_element_type=F32)

    @pl.when(e == pl.num_programs(1) - 1)
    def _():
        y = acc_sc[...].T
        o_ref[...] = x1_ref[...] + mod_ref[0, 5:6, :] * (_rms(y) * g_ref[...])


def _peer_main(h2t, u, vt, cnt, c1, rank, e2, x1, mod, g, seq, tm, eb):
    d, t = h2t.shape
    hh, nk = PEER_HEADS, PEER_NKEYS
    per_b = seq // tm
    once = pl.Buffered(1)
    sc = pl.BlockSpec((hh, nk, tm), lambda i, e: (0, 0, i), pipeline_mode=once)
    return pl.pallas_call(
        functools.partial(_peer_main_kernel, eb=eb),
        out_shape=jax.ShapeDtypeStruct((t, d), F32),
        grid=(t // tm, PEER_EXPERTS // eb),
        in_specs=[pl.BlockSpec((d, tm), lambda i, e: (0, i)),
                  pl.BlockSpec((eb, d), lambda i, e: (e, 0)),
                  pl.BlockSpec((d, eb), lambda i, e: (0, e)),
                  sc, sc, sc, sc,
                  pl.BlockSpec((tm, d), lambda i, e: (i, 0), pipeline_mode=once),
                  pl.BlockSpec((1, 6, d), lambda i, e: (i // per_b, 0, 0)),
                  pl.BlockSpec((1, d), lambda i, e: (0, 0))],
        out_specs=pl.BlockSpec((tm, d), lambda i, e: (i, 0)),
        scratch_shapes=[pltpu.VMEM((d, tm), F32),
                        pltpu.VMEM((eb, tm), BF16)],
        compiler_params=_cparams(2),
        name="peer_experts",
    )(h2t, u, vt, cnt, c1, rank, e2, x1, mod, g.reshape(1, d))


def _rope_tables(seq):
    half = MLA_ROPE // 2
    inv_freq = ROPE_THETA ** (-jnp.arange(half, dtype=F32) / half)
    ang = jnp.arange(seq, dtype=F32)[:, None] * inv_freq[None, :]
    zeros = jnp.zeros((seq, LANES - MLA_ROPE), F32)
    cc = jnp.concatenate([jnp.cos(ang), jnp.cos(ang), zeros], axis=-1)
    ss = jnp.concatenate([jnp.sin(ang), jnp.sin(ang), zeros], axis=-1)
    return cc, ss


def _rot_cols(w):
    half = MLA_ROPE // 2
    return jnp.concatenate([-w[..., half:], w[..., :half]], axis=-1)


def _layer(x2, mod, bsz, seq, w_in, g_pre_mix, g_q_lat, w_uq, g_kv_lat, w_ukv, g_out_mla,
           g_out_moba, w_out, g_post_mix, g_pre_ffn, w_peer_q, sub_keys_1, sub_keys_2,
           u_experts, v_experts, g_post_ffn):
    t = bsz * seq
    tm = min(512, seq)

    o_q, o_kv, o_kr, o_mq, o_mk, o_mv = 0, 768, 1280, 1344, 2368, 3392
    w_ql = w_in[:, o_q:o_kv].astype(BF16)
    w_kr = w_in[:, o_kr:o_mq]
    w_kvl = jnp.concatenate([w_in[:, o_kv:o_kr], w_kr, _rot_cols(w_kr)], axis=-1).astype(BF16)
    moba_scale = LOG2E / math.sqrt(MOBA_HEAD_DIM)
    w_m = jnp.concatenate([w_in[:, o_mq:o_mk] * moba_scale, w_in[:, o_mk:]], axis=-1).astype(BF16)
    wq3 = w_uq.reshape(MLA_Q_RANK, MLA_HEADS, MLA_QK) * (LOG2E / math.sqrt(MLA_QK))
    wq_rope = wq3[..., MLA_NOPE:]
    w_uq_p = jnp.concatenate([wq3[..., :MLA_NOPE], wq_rope, _rot_cols(wq_rope)], axis=-1)
    w_uq_p = w_uq_p.reshape(MLA_Q_RANK, MLA_HEADS * MLA_HEAD_PAD).astype(BF16)
    wkv3 = w_ukv.reshape(MLA_KV_RANK, MLA_HEADS, MLA_NOPE + MLA_V)
    w_ukv_p = jnp.concatenate([wkv3[..., :MLA_NOPE].reshape(MLA_KV_RANK, -1),
                               wkv3[..., MLA_NOPE:].reshape(MLA_KV_RANK, -1)], axis=-1).astype(BF16)
    cc, ss = _rope_tables(seq)
    slopes = LOG2E * 2.0 ** (-8.0 * (jnp.arange(MOBA_HEADS, dtype=F32) + 1.0) / MOBA_HEADS)
    slopes = jnp.broadcast_to(slopes[:, None, None], (MOBA_HEADS, 1, MOBA_BLOCK))

    h = _prenorm(x2, g_pre_mix, mod, seq, tm)
    q = _mla_q(h, w_ql, g_q_lat, w_uq_p, cc, ss, seq, tm)
    k, vt = _mla_kv(h, w_kvl, g_kv_lat, w_ukv_p, cc, ss, bsz, seq, tm)
    qkv_m = _matmul(h, w_m, min(1024, t), 1024, "moba_qkv_proj")
    o_mla = _mla_attn(q, k, vt, bsz, seq, min(1024, seq))
    n_blk = seq // MOBA_BLOCK
    vt_m = qkv_m[:, 2 * MOBA_WIDTH:].reshape(bsz, n_blk, MOBA_BLOCK, MOBA_HEADS, MOBA_HEAD_DIM)
    vt_m = vt_m.transpose(0, 3, 1, 4, 2)
    o_moba = _moba_attn(qkv_m, vt_m, slopes, bsz, seq, MOBA_GROUP, MOBA_SUB)
    x1, h2, h2t = _outproj(o_mla, o_moba, g_out_mla, g_out_moba, w_out.astype(BF16), g_post_mix,
                           x2, mod, g_pre_ffn, seq, tm)

    pq = _matmul(h2, w_peer_q.astype(BF16), min(1024, t), 1024, "peer_q_proj")
    cnt, c1, rank, e2 = _peer_select(pq, sub_keys_1.astype(BF16), sub_keys_2.astype(BF16),
                                     min(256, t))
    return _peer_main(h2t, u_experts.astype(BF16), v_experts.T.astype(BF16), cnt, c1, rank, e2, x1,
                      mod, g_post_ffn, seq, tm, 1024)


def kernel(x, c, w_ada, b_ada, g_pre_mix, w_in, g_q_lat, w_uq, g_kv_lat, w_ukv, g_out_mla,
           g_out_moba, w_out, g_post_mix, g_pre_ffn, w_peer_q, sub_keys_1, sub_keys_2, u_experts,
           v_experts, g_post_ffn):
    bsz, seq, d = x.shape
    x2 = x.reshape(bsz * seq, d)
    for l in range(w_ada.shape[0]):
        mod = _ada(c, w_ada[l], b_ada[l])
        x2 = _layer(x2, mod, bsz, seq, w_in[l], g_pre_mix[l], g_q_lat[l], w_uq[l], g_kv_lat[l],
                    w_ukv[l], g_out_mla[l], g_out_moba[l], w_out[l], g_post_mix[l], g_pre_ffn[l],
                    w_peer_q[l], sub_keys_1[l], sub_keys_2[l], u_experts[l], v_experts[l],
                    g_post_ffn[l])
    return x2.reshape(bsz, seq, d)
```

```python
import functools
import math

import jax
import jax.numpy as jnp
from jax import lax
from jax.experimental import pallas as pl
from jax.experimental.pallas import tpu as pltpu

F32 = jnp.float32
BF16 = jnp.bfloat16

D_MODEL = 2048
MLA_HEADS = 8
MLA_NOPE = 128
MLA_ROPE = 64
MLA_QK = MLA_NOPE + MLA_ROPE
MLA_V = 128
MLA_Q_RANK = 768
MLA_KV_RANK = 512
MLA_HEAD_PAD = 256
ROPE_THETA = 10000.0
MOBA_HEADS = 8
MOBA_HEAD_DIM = 128
MOBA_WIDTH = MOBA_HEADS * MOBA_HEAD_DIM
MOBA_BLOCK = 256
MOBA_TOPK = 3
PEER_HEADS = 8
PEER_NKEYS = 128
PEER_EXPERTS = PEER_NKEYS * PEER_NKEYS
PEER_DKEY = 256
PEER_TOPK = 16
EPS = 1e-6
NEG = -1e30
LOG2E = 1.4426950408889634
MOBA_GROUP = 8
MOBA_SUB = 2
KEY_CHUNK = 512
LANES = 128
VMEM_LIMIT = 56 * 1024 * 1024


def _cparams(n_axes, vmem=VMEM_LIMIT):
    return pltpu.CompilerParams(dimension_semantics=("arbitrary",) * n_axes,
                                vmem_limit_bytes=vmem)


def _rms(x):
    return x * lax.rsqrt(jnp.mean(x * x, axis=-1, keepdims=True) + EPS)


def _ada_kernel(c_ref, w_ref, b_ref, o_ref):
    c = c_ref[...]
    cs = c / (1.0 + jnp.exp(-c))
    o_ref[...] = jnp.dot(cs, w_ref[...], preferred_element_type=F32) + b_ref[...]


def _ada(c, w, b):
    bsz, d = c.shape
    n = w.shape[1]
    rows = 8
    cp = jnp.zeros((rows, d), F32).at[:bsz].set(c)
    tn = 1024
    out = pl.pallas_call(
        _ada_kernel,
        out_shape=jax.ShapeDtypeStruct((rows, n), F32),
        grid=(n // tn,),
        in_specs=[pl.BlockSpec((rows, d), lambda j: (0, 0)),
                  pl.BlockSpec((d, tn), lambda j: (0, j)),
                  pl.BlockSpec((1, tn), lambda j: (0, j))],
        out_specs=pl.BlockSpec((rows, tn), lambda j: (0, j)),
        compiler_params=_cparams(1),
        name="ada_mod",
    )(cp, w, b.reshape(1, n))
    return out[:bsz].reshape(bsz, 6, d)


def _prenorm_kernel(x_ref, g_ref, mod_ref, h_ref):
    y = _rms(x_ref[...]) * g_ref[...]
    h_ref[...] = (y * (1.0 + mod_ref[0, 1:2, :]) + mod_ref[0, 0:1, :]).astype(h_ref.dtype)


def _prenorm(x2, g, mod, seq, tm):
    t, d = x2.shape
    per_b = seq // tm
    return pl.pallas_call(
        _prenorm_kernel,
        out_shape=jax.ShapeDtypeStruct((t, d), BF16),
        grid=(t // tm,),
        in_specs=[pl.BlockSpec((tm, d), lambda i: (i, 0)),
                  pl.BlockSpec((1, d), lambda i: (0, 0)),
                  pl.BlockSpec((1, 6, d), lambda i: (i // per_b, 0, 0))],
        out_specs=pl.BlockSpec((tm, d), lambda i: (i, 0)),
        compiler_params=_cparams(1),
        name="prenorm_mod",
    )(x2, g.reshape(1, d), mod)


def _mm_kernel(a_ref, w_ref, o_ref):
    o_ref[...] = jnp.dot(a_ref[...], w_ref[...], preferred_element_type=F32).astype(o_ref.dtype)


def _matmul(a, w, tm, tn, name):
    t, k = a.shape
    n = w.shape[1]
    return pl.pallas_call(
        _mm_kernel,
        out_shape=jax.ShapeDtypeStruct((t, n), BF16),
        grid=(t // tm, n // tn),
        in_specs=[pl.BlockSpec((tm, k), lambda i, j: (i, 0)),
                  pl.BlockSpec((k, tn), lambda i, j: (0, j))],
        out_specs=pl.BlockSpec((tm, tn), lambda i, j: (i, j)),
        compiler_params=_cparams(2),
        name=name,
    )(a, w)


def _rope_half(r, cc, ss):
    return r * cc + pltpu.roll(r, 64, 1) * ss


def _mla_q_kernel(h_ref, wl_ref, g_ref, wu_ref, cc_ref, ss_ref, q_ref):
    ql = jnp.dot(h_ref[...], wl_ref[...], preferred_element_type=F32)
    qn = (_rms(ql) * g_ref[...]).astype(BF16)
    q2 = jnp.dot(qn, wu_ref[...], preferred_element_type=F32)
    cc = cc_ref[...]
    ss = ss_ref[...]
    for h in range(MLA_HEADS):
        base = h * MLA_HEAD_PAD
        q_ref[:, base:base + MLA_NOPE] = q2[:, base:base + MLA_NOPE].astype(BF16)
        r = q2[:, base + MLA_NOPE:base + MLA_HEAD_PAD]
        q_ref[:, base + MLA_NOPE:base + MLA_HEAD_PAD] = _rope_half(r, cc, ss).astype(BF16)


def _mla_q(h, wl, g, wu, cc, ss, seq, tm):
    t, d = h.shape
    per_b = seq // tm
    n = MLA_HEADS * MLA_HEAD_PAD
    return pl.pallas_call(
        _mla_q_kernel,
        out_shape=jax.ShapeDtypeStruct((t, n), BF16),
        grid=(t // tm,),
        in_specs=[pl.BlockSpec((tm, d), lambda i: (i, 0)),
                  pl.BlockSpec((d, MLA_Q_RANK), lambda i: (0, 0)),
                  pl.BlockSpec((1, MLA_Q_RANK), lambda i: (0, 0)),
                  pl.BlockSpec((MLA_Q_RANK, n), lambda i: (0, 0)),
                  pl.BlockSpec((tm, LANES), lambda i: (i % per_b, 0)),
                  pl.BlockSpec((tm, LANES), lambda i: (i % per_b, 0))],
        out_specs=pl.BlockSpec((tm, n), lambda i: (i, 0)),
        compiler_params=_cparams(1),
        name="mla_q_path",
    )(h, wl, g.reshape(1, MLA_Q_RANK), wu, cc, ss)


def _mla_kv_kernel(h_ref, wl_ref, g_ref, wu_ref, cc_ref, ss_ref, k_ref, vt_ref):
    kl = jnp.dot(h_ref[...], wl_ref[...], preferred_element_type=F32)
    kvn = (_rms(kl[:, :MLA_KV_RANK]) * g_ref[...]).astype(BF16)
    kr = _rope_half(kl[:, MLA_KV_RANK:], cc_ref[...], ss_ref[...]).astype(BF16)
    kv2 = jnp.dot(kvn, wu_ref[...], preferred_element_type=F32)
    for h in range(MLA_HEADS):
        base = h * MLA_HEAD_PAD
        k_ref[:, base:base + MLA_NOPE] = kv2[:, h * MLA_NOPE:(h + 1) * MLA_NOPE].astype(BF16)
        k_ref[:, base + MLA_NOPE:base + MLA_HEAD_PAD] = kr
    for c in range(vt_ref.shape[2]):
        v = kv2[c * KEY_CHUNK:(c + 1) * KEY_CHUNK, MLA_HEADS * MLA_NOPE:]
        vt_ref[0, :, c] = v.T.reshape(MLA_HEADS, MLA_V, KEY_CHUNK).astype(BF16)


def _mla_kv(h, wl, g, wu, cc, ss, bsz, seq, tm):
    t, d = h.shape
    per_b = seq // tm
    cpt = tm // KEY_CHUNK
    nl = MLA_KV_RANK + LANES
    nk = MLA_HEADS * MLA_HEAD_PAD
    vt_shape = (bsz, MLA_HEADS, seq // KEY_CHUNK, MLA_V, KEY_CHUNK)
    return pl.pallas_call(
        _mla_kv_kernel,
        out_shape=(jax.ShapeDtypeStruct((t, nk), BF16), jax.ShapeDtypeStruct(vt_shape, BF16)),
        grid=(t // tm,),
        in_specs=[pl.BlockSpec((tm, d), lambda i: (i, 0)),
                  pl.BlockSpec((d, nl), lambda i: (0, 0)),
                  pl.BlockSpec((1, MLA_KV_RANK), lambda i: (0, 0)),
                  pl.BlockSpec((MLA_KV_RANK, nk), lambda i: (0, 0)),
                  pl.BlockSpec((tm, LANES), lambda i: (i % per_b, 0)),
                  pl.BlockSpec((tm, LANES), lambda i: (i % per_b, 0))],
        out_specs=(pl.BlockSpec((tm, nk), lambda i: (i, 0)),
                   pl.BlockSpec((1, MLA_HEADS, cpt, MLA_V, KEY_CHUNK),
                                lambda i: (i // per_b, 0, i % per_b, 0, 0))),
        compiler_params=_cparams(1),
        name="mla_kv_path",
    )(h, wl, g.reshape(1, MLA_KV_RANK), wu, cc, ss)


def _online_update_t(carry, parts, vts, offset):
    m, l, acc = carry
    mb = functools.reduce(jnp.maximum, [jnp.max(x, axis=0, keepdims=True) for x in parts])
    m_new = jnp.maximum(m, mb + offset)
    alpha = jnp.exp2(m - m_new)
    shift = offset - m_new
    l = alpha * l
    acc = alpha * acc
    for x, vt in zip(parts, vts):
        p = jnp.exp2(x + shift)
        l = l + jnp.sum(p, axis=0, keepdims=True)
        acc = acc + jnp.dot(vt, p.astype(BF16), preferred_element_type=F32)
    return m_new, l, acc


def _attn_init(tq, dv):
    return (jnp.full((1, tq), NEG, F32), jnp.zeros((1, tq), F32), jnp.zeros((dv, tq), F32))


def _mla_attn_kernel(q_ref, k_ref, vt_ref, o_ref, *, tq):
    C = KEY_CHUNK
    per = tq // C
    i = pl.program_id(2)
    qt = q_ref[...].astype(F32).T.astype(BF16)
    zero = jnp.zeros((1, tq), F32)

    def scores(c):
        k = k_ref[pl.ds(pl.multiple_of(c * C, C), C), :]
        return jnp.dot(k, qt, preferred_element_type=F32)

    def trip(t, carry):
        tiles = [scores(t * per + u) for u in range(per)]
        for u in range(per):
            carry = _online_update_t(carry, [tiles[u]], [vt_ref[0, 0, t * per + u]], zero)
        return carry

    carry = lax.fori_loop(0, i, trip, _attn_init(tq, MLA_V))
    key = lax.broadcasted_iota(jnp.int32, (C, tq), 0)
    qry = lax.broadcasted_iota(jnp.int32, (C, tq), 1)
    for u in range(per):
        c = i * per + u
        s = jnp.where(key + u * C <= qry, scores(c), NEG)
        carry = _online_update_t(carry, [s], [vt_ref[0, 0, c]], zero)
    _, l, acc = carry
    o_ref[...] = (acc / l).T.astype(o_ref.dtype)


def _mla_attn(q, k, vt, bsz, seq, tq):
    nq = seq // tq
    nc = seq // KEY_CHUNK
    return pl.pallas_call(
        functools.partial(_mla_attn_kernel, tq=tq),
        out_shape=jax.ShapeDtypeStruct((bsz * seq, MLA_HEADS * MLA_V), BF16),
        grid=(bsz, MLA_HEADS, nq),
        in_specs=[pl.BlockSpec((tq, MLA_HEAD_PAD), lambda b, h, i: (b * nq + i, h)),
                  pl.BlockSpec((seq, MLA_HEAD_PAD), lambda b, h, i: (b, h)),
                  pl.BlockSpec((1, 1, nc, MLA_V, KEY_CHUNK), lambda b, h, i: (b, h, 0, 0, 0))],
        out_specs=pl.BlockSpec((tq, MLA_V), lambda b, h, i: (b * nq + i, h)),
        compiler_params=_cparams(3),
        name="mla_attention",
    )(q, k, vt)


def _moba_kernel(q_ref, k_ref, vt_ref, slope_ref, o_ref, kmean_sc, sel_sc, bias_sc, *,
                 n_blocks, group, sub):
    L = MOBA_BLOCK
    nbp = kmean_sc.shape[0]
    i = pl.program_id(2)
    slope = slope_ref[0]

    @pl.when(i == 0)
    def _():
        kmean_sc[...] = jnp.zeros_like(kmean_sc)
        for n in range(n_blocks):
            kb = k_ref[n * L:(n + 1) * L, :].astype(F32)
            kmean_sc[n:n + 1, :] = jnp.sum(kb, axis=0, keepdims=True) * (1.0 / L)
        bias_sc[...] = slope * lax.broadcasted_iota(jnp.int32, bias_sc.shape, 0).astype(F32)

    qt = q_ref[...].astype(F32).T.astype(BF16)
    blk = lax.broadcasted_iota(jnp.int32, (nbp, L), 0)
    gate = jnp.dot(kmean_sc[...].astype(BF16), qt, preferred_element_type=F32)
    gate = jnp.where(blk < i, gate, NEG)
    sel = jnp.zeros((nbp, L), F32)
    for _ in range(MOBA_TOPK):
        m = jnp.max(gate, axis=0, keepdims=True)
        first = jnp.min(jnp.where(gate == m, blk, nbp), axis=0, keepdims=True)
        hit = blk == first
        sel = jnp.where(hit, jnp.where(m > 0.5 * NEG, 1.0, 0.0), sel)
        gate = jnp.where(hit, -jnp.inf, gate)
    sel_sc[...] = sel

    def scores(b0, n):
        k = k_ref[pl.ds(pl.multiple_of(b0 * L, L), n * L), :]
        return jnp.dot(k, qt, preferred_element_type=F32) + bias_sc[0:n * L, :]

    key = lax.broadcasted_iota(jnp.int32, (L, L), 0)
    qry = lax.broadcasted_iota(jnp.int32, (L, L), 1)
    s = jnp.where(key <= qry, scores(i, 1), NEG)
    carry = _online_update_t(_attn_init(L, MOBA_HEAD_DIM), [s], [vt_ref[0, 0, i]],
                             jnp.zeros((1, L), F32))

    def trip(g, carry):
        tiles = [scores(g * group + u * sub, sub) for u in range(group // sub)]
        for u in range(group // sub):
            b0 = g * group + u * sub
            s = tiles[u]
            parts = [jnp.where(sel_sc[pl.ds(b0 + jj, 1), :] > 0.5, s[jj * L:(jj + 1) * L, :], NEG)
                     for jj in range(sub)]
            vts = [vt_ref[0, 0, b0 + jj] for jj in range(sub)]
            carry = _online_update_t(carry, parts, vts, slope * ((b0 - i) * L).astype(F32))
        return carry

    _, l, acc = lax.fori_loop(0, (i + group - 1) // group, trip, carry)
    o_ref[...] = (acc / l).T.astype(o_ref.dtype)


def _moba_attn(qkv, vt, slopes, bsz, seq, group, sub):
    L = MOBA_BLOCK
    nq = seq // L
    hh = MOBA_HEADS
    nbp = -(-nq // 16) * 16
    assert nq % group == 0 and group % sub == 0
    return pl.pallas_call(
        functools.partial(_moba_kernel, n_blocks=nq, group=group, sub=sub),
        out_shape=jax.ShapeDtypeStruct((bsz * seq, MOBA_WIDTH), BF16),
        grid=(bsz, hh, nq),
        in_specs=[pl.BlockSpec((L, MOBA_HEAD_DIM), lambda b, h, i: (b * nq + i, h)),
                  pl.BlockSpec((seq, MOBA_HEAD_DIM), lambda b, h, i: (b, hh + h)),
                  pl.BlockSpec((1, 1, nq, MOBA_HEAD_DIM, L), lambda b, h, i: (b, h, 0, 0, 0)),
                  pl.BlockSpec((1, 1, L), lambda b, h, i: (h, 0, 0))],
        out_specs=pl.BlockSpec((L, MOBA_HEAD_DIM), lambda b, h, i: (b * nq + i, h)),
        scratch_shapes=[pltpu.VMEM((nbp, MOBA_HEAD_DIM), F32),
                        pltpu.VMEM((nbp, L), F32),
                        pltpu.VMEM((sub * L, L), F32)],
        compiler_params=_cparams(3),
        name="moba_attention",
    )(qkv, qkv, vt, slopes)


def _outproj_kernel(oa_ref, ob_ref, ga_ref, gb_ref, w_ref, gpost_ref, x_ref, mod_ref, gffn_ref,
                    x1_ref, h2_ref, h2t_ref):
    half = oa_ref.shape[1]
    a = (_rms(oa_ref[...].astype(F32)) * ga_ref[...]).astype(BF16)
    b = (_rms(ob_ref[...].astype(F32)) * gb_ref[...]).astype(BF16)
    y = jnp.dot(a, w_ref[:half, :], preferred_element_type=F32)
    y = y + jnp.dot(b, w_ref[half:, :], preferred_element_type=F32)
    x1 = x_ref[...] + mod_ref[0, 2:3, :] * (_rms(y) * gpost_ref[...])
    x1_ref[...] = x1
    h2 = _rms(x1) * gffn_ref[...] * (1.0 + mod_ref[0, 4:5, :]) + mod_ref[0, 3:4, :]
    h2_ref[...] = h2.astype(BF16)
    h2t_ref[...] = h2.T.astype(BF16)


def _outproj(oa, ob, ga, gb, w, gpost, x2, mod, gffn, seq, tm):
    t, d = x2.shape
    half = oa.shape[1]
    per_b = seq // tm
    row = lambda n: pl.BlockSpec((1, n), lambda i: (0, 0))
    return pl.pallas_call(
        _outproj_kernel,
        out_shape=(jax.ShapeDtypeStruct((t, d), F32), jax.ShapeDtypeStruct((t, d), BF16),
                   jax.ShapeDtypeStruct((d, t), BF16)),
        grid=(t // tm,),
        in_specs=[pl.BlockSpec((tm, half), lambda i: (i, 0)),
                  pl.BlockSpec((tm, half), lambda i: (i, 0)),
                  row(half), row(half),
                  pl.BlockSpec((2 * half, d), lambda i: (0, 0)),
                  row(d),
                  pl.BlockSpec((tm, d), lambda i: (i, 0)),
                  pl.BlockSpec((1, 6, d), lambda i: (i // per_b, 0, 0)),
                  row(d)],
        out_specs=(pl.BlockSpec((tm, d), lambda i: (i, 0)),
                   pl.BlockSpec((tm, d), lambda i: (i, 0)),
                   pl.BlockSpec((d, tm), lambda i: (0, i))),
        compiler_params=_cparams(1),
        name="out_proj",
    )(oa, ob, ga.reshape(1, half), gb.reshape(1, half), w, gpost.reshape(1, d), x2, mod,
      gffn.reshape(1, d))


def _qk(q, k):
    return lax.dot_general(q, k, (((1,), (1,)), ((), ())), preferred_element_type=F32)


def _peer_select_kernel(h2_ref, wq_ref, k1_ref, k2_ref, cnt_ref, c1_ref, rank_ref, e2_ref, q_ref,
                        top_sc, s1_sc, s2_sc, rank_sc):
    q_ref[...] = jnp.dot(h2_ref[...], wq_ref[...], preferred_element_type=F32).astype(BF16)
    tm = q_ref.shape[0]
    hh, kk, nk = PEER_HEADS, PEER_TOPK, PEER_NKEYS
    rowi = lax.broadcasted_iota(jnp.int32, (nk, tm), 0)

    def scores(h, p):
        c0 = (2 * h + p) * (PEER_DKEY // 2)
        kref = k1_ref if p == 0 else k2_ref
        return _qk(kref[h], q_ref[:, c0:c0 + PEER_DKEY // 2])

    clean = jnp.ones((1, tm), F32)
    for h in range(hh):
        for p in range(2):
            s = scores(h, p)
            if p == 0:
                s1_sc[h] = s
            else:
                s2_sc[h] = s
            rank = jnp.full((nk, tm), float(kk), F32)
            for a in range(kk):
                m = jnp.max(s, axis=0, keepdims=True)
                top_sc[p, a, h:h + 1, :] = m
                hit = s == m
                s = jnp.where(hit, -jnp.inf, s)
                if p == 1:
                    rank = jnp.where(hit, float(a), rank)
            if p == 1:
                rank_sc[h] = rank
            gone = jnp.sum(jnp.where(s == -jnp.inf, 1.0, 0.0), axis=0, keepdims=True)
            clean = jnp.where(gone == float(kk), clean, 0.0)

    @pl.when(jnp.min(clean) < 0.5)
    def _():
        for h in range(hh):
            for p in range(2):
                s = s1_sc[h] if p == 0 else s2_sc[h]
                rank = jnp.full((nk, tm), float(kk), F32)
                for a in range(kk):
                    m = jnp.max(s, axis=0, keepdims=True)
                    top_sc[p, a, h:h + 1, :] = m
                    first = jnp.min(jnp.where(s == m, rowi, nk), axis=0, keepdims=True)
                    hit = rowi == first
                    s = jnp.where(hit, -jnp.inf, s)
                    if p == 1:
                        rank = jnp.where(hit, float(a), rank)
                if p == 1:
                    rank_sc[h] = rank

    v1 = [top_sc[0, a] for a in range(kk)]
    v2 = [top_sc[1, a] for a in range(kk)]
    cands = [v1[a] + v2[b] for a in range(kk) for b in range(kk) if (a + 1) * (b + 1) <= kk]
    best = v1[0] + v2[0]
    z = jnp.zeros_like(best)
    cur = best
    for r in range(kk):
        cur = functools.reduce(jnp.maximum, cands)
        z = z + jnp.exp(cur - best)
        if r + 1 < kk:
            todo = jnp.ones_like(best)
            nxt = []
            for c in cands:
                hit = jnp.where(c == cur, todo, 0.0)
                nxt.append(jnp.where(hit > 0.5, -jnp.inf, c))
                todo = todo - hit
            cands = nxt
    inv_z = 1.0 / z
    for h in range(hh):
        s1 = s1_sc[h]
        s2 = s2_sc[h]
        tau = cur[h:h + 1, :]
        cnt = jnp.zeros((nk, tm), F32)
        for b in range(kk):
            cnt = cnt + jnp.where(s1 + v2[b][h:h + 1, :] >= tau, 1.0, 0.0)
        cnt_ref[h] = cnt
        rank_ref[h] = rank_sc[h].astype(BF16)
        c1_ref[h] = jnp.exp(s1 - v1[0][h:h + 1, :]) * inv_z[h:h + 1, :]
        e2_ref[h] = jnp.exp(s2 - v2[0][h:h + 1, :]).astype(BF16)


def _peer_select(h2, wq, k1, k2, tm):
    t, d = h2.shape
    nq = wq.shape[1]
    hh, nk, kk = PEER_HEADS, PEER_NKEYS, PEER_TOPK
    keys = pl.BlockSpec((hh, nk, PEER_DKEY // 2), lambda i: (0, 0, 0))
    sc = pl.BlockSpec((hh, nk, tm), lambda i: (0, 0, i))
    table = jax.ShapeDtypeStruct((hh, nk, t), F32)
    table16 = jax.ShapeDtypeStruct((hh, nk, t), BF16)
    return pl.pallas_call(
        _peer_select_kernel,
        out_shape=(table, table, table16, table16),
        grid=(t // tm,),
        in_specs=[pl.BlockSpec((tm, d), lambda i: (i, 0)),
                  pl.BlockSpec((d, nq), lambda i: (0, 0), pipeline_mode=pl.Buffered(1)),
                  keys, keys],
        out_specs=(sc, sc, sc, sc),
        scratch_shapes=[pltpu.VMEM((tm, nq), BF16),
                        pltpu.VMEM((2, kk, hh, tm), F32),
                        pltpu.VMEM((hh, nk, tm), F32),
                        pltpu.VMEM((hh, nk, tm), F32),
                        pltpu.VMEM((hh, nk, tm), F32)],
        compiler_params=_cparams(1),
        name="peer_select",
    )(h2, wq, k1, k2)


def _peer_main_kernel(h2t_ref, u_ref, vt_ref, cnt_ref, c1_ref, rank_ref, e2_ref, x1_ref, mod_ref,
                      g_ref, o_ref, acc_sc, p_sc, *, eb):
    hh, nk = PEER_HEADS, PEER_NKEYS
    e = pl.program_id(1)
    groups = eb // nk

    @pl.when(e == 0)
    def _():
        acc_sc[...] = jnp.zeros_like(acc_sc)

    a = jnp.dot(u_ref[...], h2t_ref[...], preferred_element_type=F32)
    for g in range(groups):
        i1 = e * groups + g
        w = jnp.zeros((nk, a.shape[1]), BF16)
        for h in range(hh):
            picked = rank_ref[h] < cnt_ref[h, pl.ds(i1, 1), :].astype(BF16)
            e2 = e2_ref[h]
            w = w + jnp.where(picked, e2, jnp.zeros_like(e2)) * c1_ref[h, pl.ds(i1, 1), :].astype(BF16)
        ag = a[g * nk:(g + 1) * nk, :]
        gelu = 0.5 * ag * (1.0 + lax.erf(ag * (1.0 / math.sqrt(2.0))))
        p_sc[g * nk:(g + 1) * nk, :] = (gelu * w.astype(F32)).astype(BF16)
    acc_sc[...] += jnp.dot(vt_ref[...], p_sc[...], preferred_element_type=F32)

    @pl.when(e == pl.num_programs(1) - 1)
    def _():
        y = acc_sc[...].T
        o_ref[...] = x1_ref[...] + mod_ref[0, 5:6, :] * (_rms(y) * g_ref[...])


def _peer_main(h2t, u, vt, cnt, c1, rank, e2, x1, mod, g, seq, tm, eb):
    d, t = h2t.shape
    hh, nk = PEER_HEADS, PEER_NKEYS
    per_b = seq // tm
    once = pl.Buffered(1)
    sc = pl.BlockSpec((hh, nk, tm), lambda i, e: (0, 0, i), pipeline_mode=once)
    return pl.pallas_call(
        functools.partial(_peer_main_kernel, eb=eb),
        out_shape=jax.ShapeDtypeStruct((t, d), F32),
        grid=(t // tm, PEER_EXPERTS // eb),
        in_specs=[pl.BlockSpec((d, tm), lambda i, e: (0, i)),
                  pl.BlockSpec((eb, d), lambda i, e: (e, 0)),
                  pl.BlockSpec((d, eb), lambda i, e: (0, e)),
                  sc, sc, sc, sc,
                  pl.BlockSpec((tm, d), lambda i, e: (i, 0), pipeline_mode=once),
                  pl.BlockSpec((1, 6, d), lambda i, e: (i // per_b, 0, 0)),
                  pl.BlockSpec((1, d), lambda i, e: (0, 0))],
        out_specs=pl.BlockSpec((tm, d), lambda i, e: (i, 0)),
        scratch_shapes=[pltpu.VMEM((d, tm), F32),
                        pltpu.VMEM((eb, tm), BF16)],
        compiler_params=_cparams(2),
        name="peer_experts",
    )(h2t, u, vt, cnt, c1, rank, e2, x1, mod, g.reshape(1, d))


def _rope_tables(seq):
    half = MLA_ROPE // 2
    inv_freq = ROPE_THETA ** (-jnp.arange(half, dtype=F32) / half)
    ang = jnp.arange(seq, dtype=F32)[:, None] * inv_freq[None, :]
    zeros = jnp.zeros((seq, LANES - MLA_ROPE), F32)
    cc = jnp.concatenate([jnp.cos(ang), jnp.cos(ang), zeros], axis=-1)
    ss = jnp.concatenate([jnp.sin(ang), jnp.sin(ang), zeros], axis=-1)
    return cc, ss


def _rot_cols(w):
    half = MLA_ROPE // 2
    return jnp.concatenate([-w[..., half:], w[..., :half]], axis=-1)


def _layer(x2, mod, bsz, seq, w_in, g_pre_mix, g_q_lat, w_uq, g_kv_lat, w_ukv, g_out_mla,
           g_out_moba, w_out, g_post_mix, g_pre_ffn, w_peer_q, sub_keys_1, sub_keys_2,
           u_experts, v_experts, g_post_ffn):
    t = bsz * seq
    tm = min(512, seq)

    o_q, o_kv, o_kr, o_mq, o_mk, o_mv = 0, 768, 1280, 1344, 2368, 3392
    w_ql = w_in[:, o_q:o_kv].astype(BF16)
    w_kr = w_in[:, o_kr:o_mq]
    w_kvl = jnp.concatenate([w_in[:, o_kv:o_kr], w_kr, _rot_cols(w_kr)], axis=-1).astype(BF16)
    moba_scale = LOG2E / math.sqrt(MOBA_HEAD_DIM)
    w_m = jnp.concatenate([w_in[:, o_mq:o_mk] * moba_scale, w_in[:, o_mk:]], axis=-1).astype(BF16)
    wq3 = w_uq.reshape(MLA_Q_RANK, MLA_HEADS, MLA_QK) * (LOG2E / math.sqrt(MLA_QK))
    wq_rope = wq3[..., MLA_NOPE:]
    w_uq_p = jnp.concatenate([wq3[..., :MLA_NOPE], wq_rope, _rot_cols(wq_rope)], axis=-1)
    w_uq_p = w_uq_p.reshape(MLA_Q_RANK, MLA_HEADS * MLA_HEAD_PAD).astype(BF16)
    wkv3 = w_ukv.reshape(MLA_KV_RANK, MLA_HEADS, MLA_NOPE + MLA_V)
    w_ukv_p = jnp.concatenate([wkv3[..., :MLA_NOPE].reshape(MLA_KV_RANK, -1),
                               wkv3[..., MLA_NOPE:].reshape(MLA_KV_RANK, -1)], axis=-1).astype(BF16)
    cc, ss = _rope_tables(seq)
    slopes = LOG2E * 2.0 ** (-8.0 * (jnp.arange(MOBA_HEADS, dtype=F32) + 1.0) / MOBA_HEADS)
    slopes = jnp.broadcast_to(slopes[:, None, None], (MOBA_HEADS, 1, MOBA_BLOCK))

    h = _prenorm(x2, g_pre_mix, mod, seq, tm)
    q = _mla_q(h, w_ql, g_q_lat, w_uq_p, cc, ss, seq, tm)
    k, vt = _mla_kv(h, w_kvl, g_kv_lat, w_ukv_p, cc, ss, bsz, seq, tm)
    qkv_m = _matmul(h, w_m, min(1024, t), 1024, "moba_qkv_proj")
    o_mla = _mla_attn(q, k, vt, bsz, seq, min(1024, seq))
    n_blk = seq // MOBA_BLOCK
    vt_m = qkv_m[:, 2 * MOBA_WIDTH:].reshape(bsz, n_blk, MOBA_BLOCK, MOBA_HEADS, MOBA_HEAD_DIM)
    vt_m = vt_m.transpose(0, 3, 1, 4, 2)
    o_moba = _moba_attn(qkv_m, vt_m, slopes, bsz, seq, MOBA_GROUP, MOBA_SUB)
    x1, h2, h2t = _outproj(o_mla, o_moba, g_out_mla, g_out_moba, w_out.astype(BF16), g_post_mix,
                           x2, mod, g_pre_ffn, seq, tm)

    cnt, c1, rank, e2 = _peer_select(h2, w_peer_q.astype(BF16), sub_keys_1.astype(BF16),
                                     sub_keys_2.astype(BF16), min(256, t))
    return _peer_main(h2t, u_experts.astype(BF16), v_experts.T.astype(BF16), cnt, c1, rank, e2, x1,
                      mod, g_post_ffn, seq, tm, 1024)


def kernel(x, c, w_ada, b_ada, g_pre_mix, w_in, g_q_lat, w_uq, g_kv_lat, w_ukv, g_out_mla,
           g_out_moba, w_out, g_post_mix, g_pre_ffn, w_peer_q, sub_keys_1, sub_keys_2, u_experts,
           v_experts, g_post_ffn):
    bsz, seq, d = x.shape
    x2 = x.reshape(bsz * seq, d)
    for l in range(w_ada.shape[0]):
        mod = _ada(c, w_ada[l], b_ada[l])
        x2 = _layer(x2, mod, bsz, seq, w_in[l], g_pre_mix[l], g_q_lat[l], w_uq[l], g_kv_lat[l],
                    w_ukv[l], g_out_mla[l], g_out_moba[l], w_out[l], g_post_mix[l], g_pre_ffn[l],
                    w_peer_q[l], sub_keys_1[l], sub_keys_2[l], u_experts[l], v_experts[l],
                    g_post_ffn[l])
    return x2.reshape(bsz, seq, d)
```
